```python
import math
import jax, jax.numpy as jnp
from jax import lax
import numpy as np

D_MODEL = 1024
BATCH = 8
SEQ = 2048
DEPTH = 1

CHUNK = 64
SB_HEADS = 8
SB_HEAD_DIM = 64
SB_WIDTH = SB_HEADS * SB_HEAD_DIM
SB_BLOCK = 128
GLA_HEADS = 4
GLA_KEY_DIM = 64
GLA_VALUE_DIM = 128
GLA_K_WIDTH = GLA_HEADS * GLA_KEY_DIM
GLA_V_WIDTH = GLA_HEADS * GLA_VALUE_DIM
GLA_GATE_RANK = 16
GLA_GATE_TEMP = 16.0
N_BRANCHES = 2
N_EXPERTS = 64
TOP_K = 8
N_GROUPS = 8
TOPK_GROUPS = 4
EXPERT_DIM = 256
SHARED_DIM = 256
ROUTED_SCALE = 2.5
DN_ALPHA = (2 * DEPTH) ** 0.25
DN_BETA = (8 * DEPTH) ** -0.25
LN_EPS = 1e-5
IN_WIDTHS = (SB_WIDTH, SB_WIDTH, SB_WIDTH, GLA_K_WIDTH, GLA_K_WIDTH, GLA_V_WIDTH, GLA_V_WIDTH,
             GLA_GATE_RANK, N_BRANCHES * D_MODEL)
IN_WIDTH = 3 * SB_WIDTH + 2 * GLA_K_WIDTH + 2 * GLA_V_WIDTH + GLA_GATE_RANK + N_BRANCHES * D_MODEL

kernel_name = "hybrid_sb_gla_moe_deepnorm_adaln"


def _split_points():
    pts, acc = [], 0
    for w in IN_WIDTHS[:-1]:
        acc += w
        pts.append(acc)
    return pts


def layer_norm(x, g=None, b=None):
    xf = x.astype(jnp.float32)
    mu = jnp.mean(xf, axis=-1, keepdims=True)
    var = jnp.mean(jnp.square(xf - mu), axis=-1, keepdims=True)
    y = (xf - mu) * lax.rsqrt(var + LN_EPS)
    if g is not None:
        y = y * g.astype(jnp.float32) + b.astype(jnp.float32)
    return y.astype(x.dtype)


def split_heads(t, n_heads):
    b, s, w = t.shape
    return t.reshape(b, s, n_heads, w // n_heads).transpose(0, 2, 1, 3)


def merge_heads(t):
    b, h, s, d = t.shape
    return t.transpose(0, 2, 1, 3).reshape(b, s, h * d)


def stick_breaking_attention(q, k, v):
    q, k, v = (t.astype(jnp.float32) for t in (q, k, v))
    seq = q.shape[2]
    scale = 1.0 / math.sqrt(q.shape[-1])
    outs = []
    for i in range(seq // SB_BLOCK):
        kv_len = (i + 1) * SB_BLOCK
        q_blk = q[:, :, i * SB_BLOCK:kv_len]
        k_blk, v_blk = k[:, :, :kv_len], v[:, :, :kv_len]
        z = jnp.einsum('bhqd,bhkd->bhqk', q_blk, k_blk) * scale
        q_pos = i * SB_BLOCK + jnp.arange(SB_BLOCK)
        k_pos = jnp.arange(kv_len)
        strict = k_pos[None, :] < q_pos[:, None]
        log_beta = jnp.where(strict, jax.nn.log_sigmoid(z), -jnp.inf)
        log_1m = jnp.where(strict, jax.nn.log_sigmoid(-z), 0.0)
        rest = lax.cumsum(log_1m, axis=3, reverse=True) - log_1m
        w = jnp.exp(log_beta + rest)
        outs.append(jnp.einsum('bhqk,bhkd->bhqd', w, v_blk))
    return jnp.concatenate(outs, axis=2)


def gated_linear_attention(q, k, v, log_a):
    q, k, v, log_a = (t.astype(jnp.float32) for t in (q, k, v, log_a))
    b, h, seq, dk = q.shape
    dv = v.shape[-1]
    n_chunks = seq // CHUNK

    def to_chunks(t):
        return jnp.moveaxis(t.reshape(b, h, n_chunks, CHUNK, t.shape[-1]), 2, 0)

    qc, kc, vc = to_chunks(q), to_chunks(k), to_chunks(v)
    gc = jnp.cumsum(to_chunks(log_a), axis=3)
    causal = jnp.tril(jnp.ones((CHUNK, CHUNK), dtype=bool))

    def step(state, inp):
        q_c, k_c, v_c, g_c = inp
        o_inter = jnp.einsum('bhtd,bhdv->bhtv', q_c * jnp.exp(g_c), state)
        diff = g_c[:, :, :, None, :] - g_c[:, :, None, :, :]
        decay = jnp.exp(jnp.where(causal[:, :, None], diff, -jnp.inf))
        scores = jnp.einsum('bhtd,bhsd,bhtsd->bhts', q_c, k_c, decay)
        o_intra = jnp.einsum('bhts,bhsv->bhtv', scores, v_c)
        g_last = g_c[:, :, -1]
        k_dec = k_c * jnp.exp(g_last[:, :, None, :] - g_c)
        state = jnp.exp(g_last)[..., None] * state + jnp.einsum('bhsd,bhsv->bhdv', k_dec, v_c)
        return state, o_inter + o_intra

    s0 = jnp.zeros((b, h, dk, dv), jnp.float32)
    _, o = lax.scan(step, s0, (qc, kc, vc, gc))
    return jnp.moveaxis(o, 0, 2).reshape(b, h, seq, dv)


def token_mixer(h, w_in, gla_w_gate_up, gla_b_gate, gla_norm_w, w_branch_sb, w_branch_gla, w_out):
    proj = h @ w_in
    q_sb, k_sb, v_sb, q_g, k_g, v_g, r_g, g_lr, merge = jnp.split(proj, _split_points(), axis=-1)
    o_sb = merge_heads(stick_breaking_attention(split_heads(q_sb, SB_HEADS), split_heads(k_sb, SB_HEADS),
                                                split_heads(v_sb, SB_HEADS))).astype(h.dtype)
    log_a = jax.nn.log_sigmoid((g_lr @ gla_w_gate_up + gla_b_gate).astype(jnp.float32)) / GLA_GATE_TEMP
    o_g = gated_linear_attention(split_heads(q_g, GLA_HEADS) * (GLA_KEY_DIM ** -0.5),
                                 split_heads(k_g, GLA_HEADS), split_heads(v_g, GLA_HEADS),
                                 split_heads(log_a, GLA_HEADS))
    o_g = o_g * lax.rsqrt(jnp.mean(jnp.square(o_g), axis=-1, keepdims=True) + LN_EPS)
    o_g = (merge_heads(o_g) * gla_norm_w.astype(jnp.float32)).astype(h.dtype) * jax.nn.silu(r_g)
    gate_sb, gate_gla = jnp.split(jax.nn.sigmoid(merge), 2, axis=-1)
    y = gate_sb * (o_sb @ w_branch_sb) + gate_gla * (o_g @ w_branch_gla)
    return y @ w_out


def moe_ffn(h, w_router, router_bias, w_exp_gate_up, w_exp_down, w_shared_gate_up, w_shared_down):
    b, s, d = h.shape
    scores = jax.nn.sigmoid((h @ w_router).astype(jnp.float32))
    sel = scores + router_bias.astype(jnp.float32)
    grp = sel.reshape(b, s, N_GROUPS, N_EXPERTS // N_GROUPS)
    group_score = jnp.sum(lax.top_k(grp, 2)[0], axis=-1)
    _, top_groups = lax.top_k(group_score, TOPK_GROUPS)
    group_mask = jnp.sum(jax.nn.one_hot(top_groups, N_GROUPS, dtype=jnp.float32), axis=-2)
    expert_mask = jnp.repeat(group_mask, N_EXPERTS // N_GROUPS, axis=-1)
    _, idx = lax.top_k(jnp.where(expert_mask > 0, sel, -jnp.inf), TOP_K)
    w = jnp.take_along_axis(scores, idx, axis=-1)
    w = w / jnp.sum(w, axis=-1, keepdims=True) * ROUTED_SCALE
    gates = jnp.sum(jax.nn.one_hot(idx, N_EXPERTS, dtype=jnp.float32) * w[..., None], axis=-2)
    gates = gates.astype(h.dtype)

    def routed_row(args):
        hr, gr = args
        gu = jnp.einsum('td,edf->tef', hr, w_exp_gate_up)
        g_part, u_part = jnp.split(gu, 2, axis=-1)
        act = jax.nn.silu(g_part) * u_part * gr[..., None]
        return jnp.einsum('tef,efd->td', act, w_exp_down)

    routed = lax.map(routed_row, (h, gates))
    sg, su = jnp.split(h @ w_shared_gate_up, 2, axis=-1)
    shared = (jax.nn.silu(sg) * su) @ w_shared_down
    return routed + shared


def setup_inputs(seed: int = 0) -> dict:
    key = jax.random.key(seed)
    ks = jax.random.split(key, 24)

    def nrm(k, shape, scale):
        return jax.random.normal(k, shape, jnp.float32) * scale

    L, D, E = DEPTH, D_MODEL, N_EXPERTS
    return {
        "x": nrm(ks[0], (BATCH, SEQ, D), 1.0),
        "c": nrm(ks[1], (BATCH, D), 1.0),
        "w_ada": nrm(ks[2], (L, D, 6 * D), 0.5 * D ** -0.5),
        "b_ada": nrm(ks[3], (L, 6 * D), 0.02),
        "w_in": nrm(ks[4], (L, D, IN_WIDTH), D ** -0.5),
        "gla_w_gate_up": nrm(ks[5], (L, GLA_GATE_RANK, GLA_K_WIDTH), GLA_GATE_RANK ** -0.5),
        "gla_b_gate": 1.0 + nrm(ks[6], (L, GLA_K_WIDTH), 0.1),
        "gla_norm_w": 1.0 + nrm(ks[7], (L, GLA_V_WIDTH), 0.02),
        "w_branch_sb": nrm(ks[8], (L, SB_WIDTH, D), SB_WIDTH ** -0.5),
        "w_branch_gla": nrm(ks[9], (L, GLA_V_WIDTH, D), GLA_V_WIDTH ** -0.5),
        "w_out": nrm(ks[10], (L, D, D), DN_BETA * D ** -0.5),
        "ln1_g": 1.0 + nrm(ks[11], (L, D), 0.02),
        "ln1_b": nrm(ks[12], (L, D), 0.02),
        "w_router": nrm(ks[13], (L, D, E), D ** -0.5),
        "router_bias": nrm(ks[14], (L, E), 0.01),
        "w_exp_gate_up": nrm(ks[15], (L, E, D, 2 * EXPERT_DIM), D ** -0.5),
        "w_exp_down": nrm(ks[16], (L, E, EXPERT_DIM, D), DN_BETA * EXPERT_DIM ** -0.5),
        "w_shared_gate_up": nrm(ks[17], (L, D, 2 * SHARED_DIM), D ** -0.5),
        "w_shared_down": nrm(ks[18], (L, SHARED_DIM, D), DN_BETA * SHARED_DIM ** -0.5),
        "ln2_g": 1.0 + nrm(ks[19], (L, D), 0.02),
        "ln2_b": nrm(ks[20], (L, D), 0.02),
    }


def reference(x, c, w_ada, b_ada, w_in, gla_w_gate_up, gla_b_gate, gla_norm_w, w_branch_sb, w_branch_gla,
              w_out, ln1_g, ln1_b, w_router, router_bias, w_exp_gate_up, w_exp_down, w_shared_gate_up,
              w_shared_down, ln2_g, ln2_b):
    for l in range(DEPTH):
        mod = jax.nn.silu(c) @ w_ada[l] + b_ada[l]
        shift1, scale1, gate1, shift2, scale2, gate2 = (m[:, None, :] for m in jnp.split(mod, 6, axis=-1))
        h = layer_norm(x) * (1.0 + scale1) + shift1
        mix = token_mixer(h, w_in[l], gla_w_gate_up[l], gla_b_gate[l], gla_norm_w[l],
                          w_branch_sb[l], w_branch_gla[l], w_out[l])
        x = layer_norm(DN_ALPHA * x + gate1 * mix, ln1_g[l], ln1_b[l])
        h = layer_norm(x) * (1.0 + scale2) + shift2
        ffn = moe_ffn(h, w_router[l], router_bias[l], w_exp_gate_up[l], w_exp_down[l],
                      w_shared_gate_up[l], w_shared_down[l])
        x = layer_norm(DN_ALPHA * x + gate2 * ffn, ln2_g[l], ln2_b[l])
    return x
```

```python
import functools
import math

import jax
import jax.numpy as jnp
from jax import lax
from jax.experimental import pallas as pl
from jax.experimental.pallas import tpu as pltpu

F32 = jnp.float32
BF16 = jnp.bfloat16

SB_HEADS = 8
SB_HEAD_DIM = 64
GLA_HEADS = 4
GLA_KEY_DIM = 64
GLA_VALUE_DIM = 128
GLA_CHUNK = 64
GLA_GATE_RANK = 16
GLA_GATE_TEMP = 16.0
N_EXPERTS = 64
TOP_K = 8
N_GROUPS = 8
TOPK_GROUPS = 4
ROUTED_SCALE = 2.5
DEPTH = 1
DN_ALPHA = (2 * DEPTH) ** 0.25
LN_EPS = 1e-5

LANES = 128
MXU_DIM = 256
VMEM_LIMIT = 56 * 1024 * 1024


def _dot(a, b):
    return jnp.dot(a, b, preferred_element_type=F32)


def _dot_nt(a, b):
    return lax.dot_general(a, b, (((1,), (1,)), ((), ())), preferred_element_type=F32)


def _split_hi_lo(x):
    hi = x.astype(BF16)
    lo = (x - hi.astype(F32)).astype(BF16)
    return hi, lo


def _silu(x):
    return x * jax.nn.sigmoid(x)


def _layer_norm(x):
    mu = jnp.mean(x, axis=-1, keepdims=True)
    xc = x - mu
    var = jnp.mean(xc * xc, axis=-1, keepdims=True)
    return xc * lax.rsqrt(var + LN_EPS)


def _params(sem):
    return pltpu.CompilerParams(dimension_semantics=sem, vmem_limit_bytes=VMEM_LIMIT)


def _ada_kernel(c_ref, w_ref, b_ref, o_ref):
    s_hi, s_lo = _split_hi_lo(_silu(c_ref[...]))
    w_hi, w_lo = _split_hi_lo(w_ref[...])
    acc = _dot(s_hi, w_hi) + _dot(s_lo, w_hi) + _dot(s_hi, w_lo)
    o_ref[...] = acc + b_ref[...]


def _ada(c, w_ada, b_ada):
    bsz, d = c.shape
    n_out = w_ada.shape[1]
    tn = d
    return pl.pallas_call(
        _ada_kernel,
        grid=(n_out // tn,),
        in_specs=[pl.BlockSpec((bsz, d), lambda j: (0, 0)),
                  pl.BlockSpec((d, tn), lambda j: (0, j)),
                  pl.BlockSpec((1, tn), lambda j: (0, j))],
        out_specs=pl.BlockSpec((bsz, tn), lambda j: (0, j)),
        out_shape=jax.ShapeDtypeStruct((bsz, n_out), F32),
        compiler_params=_params(("parallel",)),
        name="ada",
    )(c, w_ada, b_ada.reshape(1, n_out))


def _inproj_kernel(x_ref, mod_ref, wa_ref, wglr_ref, wgu_ref, bg_ref, wm_ref,
                   qsb_ref, ksb_ref, vsb_ref, qg_ref, kg_ref, vg_ref, rg_ref, la_ref,
                   gsb_ref, ggla_ref):
    mod = mod_ref[0]
    shift1, scale1 = mod[0:1, :], mod[1:2, :]
    h = _layer_norm(x_ref[...]) * (1.0 + scale1) + shift1
    hb = h.astype(BF16)
    sbw = SB_HEADS * SB_HEAD_DIM
    gkw = GLA_HEADS * GLA_KEY_DIM
    gvw = GLA_HEADS * GLA_VALUE_DIM
    off = 0
    qsb_ref[...] = (_dot(hb, wa_ref[:, off:off + sbw]) * (SB_HEAD_DIM ** -0.5)).astype(BF16); off += sbw
    ksb_ref[...] = _dot(hb, wa_ref[:, off:off + sbw]).astype(BF16); off += sbw
    vsb_ref[...] = _dot(hb, wa_ref[:, off:off + sbw]).astype(BF16); off += sbw
    qg_ref[...] = (_dot(hb, wa_ref[:, off:off + gkw]) * (GLA_KEY_DIM ** -0.5)).astype(BF16); off += gkw
    kg_ref[...] = _dot(hb, wa_ref[:, off:off + gkw]).astype(BF16); off += gkw
    vg_ref[...] = _dot(hb, wa_ref[:, off:off + gvw]).astype(BF16); off += gvw
    rg_ref[...] = _silu(_dot(hb, wa_ref[:, off:off + gvw])).astype(BF16)
    g_lr = _dot(hb, wglr_ref[...])
    z = _dot(g_lr.astype(BF16), wgu_ref[...]) + bg_ref[...]
    la_ref[...] = (jnp.minimum(z, 0.0) - jnp.log(1.0 + jnp.exp(-jnp.abs(z)))) * (1.0 / GLA_GATE_TEMP)
    d = gsb_ref.shape[1]
    gsb_ref[...] = jax.nn.sigmoid(_dot(hb, wm_ref[:, 0:d])).astype(BF16)
    ggla_ref[...] = jax.nn.sigmoid(_dot(hb, wm_ref[:, d:2 * d])).astype(BF16)


def _inproj(x2, mod3, w_a, w_glr, w_gu, b_gate, w_merge, seq, tm):
    n, d = x2.shape
    sbw = SB_HEADS * SB_HEAD_DIM
    gkw = GLA_HEADS * GLA_KEY_DIM
    gvw = GLA_HEADS * GLA_VALUE_DIM
    const = lambda shape: pl.BlockSpec(shape, lambda i: (0,) * len(shape), pipeline_mode=pl.Buffered(1))
    row = lambda w: pl.BlockSpec((tm, w), lambda i: (i, 0))
    widths = (sbw, sbw, sbw, gkw, gkw, gvw, gvw)
    out_shape = [jax.ShapeDtypeStruct((n, w), BF16) for w in widths]
    out_shape += [jax.ShapeDtypeStruct((n, gkw), F32),
                  jax.ShapeDtypeStruct((n, d), BF16), jax.ShapeDtypeStruct((n, d), BF16)]
    out_specs = [row(w) for w in widths] + [row(gkw), row(d), row(d)]
    return pl.pallas_call(
        _inproj_kernel,
        grid=(n // tm,),
        in_specs=[row(d),
                  pl.BlockSpec((1, 6, d), lambda i: (i * tm // seq, 0, 0)),
                  const(w_a.shape), const(w_glr.shape), const(w_gu.shape), const(b_gate.shape),
                  const(w_merge.shape)],
        out_specs=out_specs,
        out_shape=out_shape,
        compiler_params=_params(("parallel",)),
        name="inproj",
    )(x2, mod3, w_a, w_glr, w_gu, b_gate, w_merge)


def _sb_kernel(q_ref, k_ref, v_ref, o_ref, *, tq):
    qi = pl.program_id(2)
    q = q_ref[...]
    lane = lax.broadcasted_iota(jnp.int32, (1, LANES), 1)
    head0 = lane < SB_HEAD_DIM
    zero = jnp.zeros_like(q)
    qs = (jnp.where(head0, q, zero), jnp.where(head0, zero, q))
    r = lax.broadcasted_iota(jnp.int32, (tq, tq), 0)
    c = lax.broadcasted_iota(jnp.int32, (tq, tq), 1)
    strict = c < r
    upper = (r > c).astype(BF16)

    def tile(j, state, diag):
        start = pl.multiple_of(j * tq, tq)
        k = k_ref[pl.ds(start, tq), :]
        v = v_ref[pl.ds(start, tq), :]
        new = []
        for h in range(2):
            acc, carry = state[h]
            z = _dot_nt(qs[h], k)
            ls_neg = jnp.minimum(-z, 0.0) - jnp.log(1.0 + jnp.exp(-jnp.abs(z)))
            log_beta = ls_neg + z
            if diag:
                ls_neg = jnp.where(strict, ls_neg, 0.0)
            hi, lo = _split_hi_lo(ls_neg)
            rest = _dot(hi, upper) + _dot(lo, upper)
            w = jnp.exp(log_beta + rest + carry)
            if diag:
                w = jnp.where(strict, w, 0.0)
            acc = acc + _dot(w.astype(BF16), v)
            carry = carry + jnp.sum(ls_neg, axis=1, keepdims=True)
            new.append((acc, carry))
        return tuple(new)

    init = tuple((jnp.zeros((tq, LANES), F32), jnp.zeros((tq, 1), F32)) for _ in range(2))
    state = tile(qi, init, True)
    state = lax.fori_loop(0, qi, lambda jj, st: tile(qi - 1 - jj, st, False), state)
    o_ref[...] = jnp.where(head0, state[0][0], state[1][0]).astype(BF16)


def _sb_attention(q, k, v, bsz, seq, tq):
    n, w = q.shape
    n_pairs = w // LANES
    nq = seq // tq
    return pl.pallas_call(
        functools.partial(_sb_kernel, tq=tq),
        grid=(bsz, n_pairs, nq),
        in_specs=[pl.BlockSpec((tq, LANES), lambda b, p, i: (b * nq + i, p)),
                  pl.BlockSpec((seq, LANES), lambda b, p, i: (b, p)),
                  pl.BlockSpec((seq, LANES), lambda b, p, i: (b, p))],
        out_specs=pl.BlockSpec((tq, LANES), lambda b, p, i: (b * nq + i, p)),
        out_shape=jax.ShapeDtypeStruct((n, w), BF16),
        compiler_params=_params(("parallel", "parallel", "arbitrary")),
        name="sb_attn",
    )(q, k, v)


GLA_SLAB = MXU_DIM
GLA_SUB = 16


def _gla_constants():
    s, c, sb = GLA_SLAB, GLA_CHUNK, GLA_SUB
    t = lax.broadcasted_iota(jnp.int32, (s, s), 0)
    u = lax.broadcasted_iota(jnp.int32, (s, s), 1)
    same_chunk = (t // c) == (u // c)

    def cum_to(row):
        return (same_chunk & (u <= row)).astype(F32)

    g = cum_to(t)
    g_total = same_chunk.astype(F32)
    sub_start = (t // sb) * sb
    next_sub_start = jnp.minimum(sub_start + sb, (t // c) * c + c - 1)
    chunk_mid = (t // c) * c + c // 2
    mats = (g,
            g_total - g,
            g - cum_to(sub_start),
            cum_to(next_sub_start) - g,
            g - cum_to(chunk_mid))
    return [m.astype(BF16) for m in mats]


def _gla_kernel(q_ref, k_ref, v_ref, la_ref, rg_ref, nw_ref, o_ref, s_ref):
    s, c, sb = GLA_SLAB, GLA_CHUNK, GLA_SUB
    dk, dv, nh = GLA_KEY_DIM, GLA_VALUE_DIM, GLA_HEADS
    kw = nh * dk
    n_slabs = q_ref.shape[0] // s
    per = s // c
    m_g, m_kdec, m_q0, m_k1, m_q2 = _gla_constants()

    t = lax.broadcasted_iota(jnp.int32, (s, s), 0)
    u = lax.broadcasted_iota(jnp.int32, (s, s), 1)
    causal = ((t // c) == (u // c)) & (u <= t)
    bt, bu = (t % c) // sb, (u % c) // sb
    lvl0 = causal & (bt == bu)
    lvl1 = causal & ((bt % 2) == 1) & (bu == bt - 1)
    lvl2 = causal & (bt >= 2) & (bu <= 1)
    chunk_of_row_is_lane_block = (t // c) == (u // dk)
    lane_block_is_chunk_of_col = (t // dk) == (u // c)
    lane_kw = lax.broadcasted_iota(jnp.int32, (1, kw), 1)
    rowi = lax.broadcasted_iota(jnp.int32, (kw, kw), 0)
    coli = lax.broadcasted_iota(jnp.int32, (kw, kw), 1)

    s_ref[...] = jnp.zeros_like(s_ref)

    def slab(i, _):
        r0 = pl.multiple_of(i * s, s)
        rows = pl.ds(r0, s)
        la_hi, la_lo = _split_hi_lo(la_ref[rows, :])

        def cum(m):
            return _dot(m, la_hi) + _dot(m, la_lo)

        q = q_ref[rows, :].astype(F32)
        k = k_ref[rows, :].astype(F32)
        e_q0 = cum(m_q0)
        e_q2 = cum(m_q2)
        qg = (q * jnp.exp(cum(m_g))).astype(BF16)
        kdec = (k * jnp.exp(cum(m_kdec))).astype(BF16)
        q0 = q * jnp.exp(e_q0)
        k0 = (k * jnp.exp(-e_q0)).astype(BF16)
        k1 = (k * jnp.exp(cum(m_k1))).astype(BF16)
        q2 = q * jnp.exp(e_q2)
        k2 = (k * jnp.exp(-e_q2)).astype(BF16)
        g_last = cum(m_g + m_kdec)
        gl_rows = jnp.concatenate(
            [jnp.broadcast_to(g_last[j * c:j * c + 1, :], (LANES, kw)) for j in range(per)], axis=0)
        gl_hi, gl_lo = _split_hi_lo(gl_rows)

        for h in range(nh):
            in_head = (lane_kw >= h * dk) & (lane_kw < (h + 1) * dk)
            sel = ((rowi // dk == h) & (rowi % dk == coli % dk)).astype(BF16)
            sel_t = ((coli // dk == h) & (coli % dk == rowi % dk)).astype(BF16)
            v = v_ref[rows, h * dv:(h + 1) * dv]

            a0 = _dot_nt(jnp.where(in_head, q0, 0.0).astype(BF16), k0)
            a1 = _dot_nt(jnp.where(in_head, q0, 0.0).astype(BF16), k1)
            a2 = _dot_nt(jnp.where(in_head, q2, 0.0).astype(BF16), k2)
            a = jnp.where(lvl0, a0, jnp.where(lvl1, a1, jnp.where(lvl2, a2, 0.0)))
            o = _dot(a.astype(BF16), v)

            q_blk = jnp.where(chunk_of_row_is_lane_block, _dot(qg, sel), 0.0).astype(BF16)
            k_blk_t = jnp.where(lane_block_is_chunk_of_col, _dot_nt(sel_t, kdec), 0.0).astype(BF16)
            upd = _dot(k_blk_t, v)
            dec = jnp.exp(_dot_nt(sel_t[0:dk, :], gl_hi) + _dot_nt(sel_t[0:dk, :], gl_lo))
            state = s_ref[h]
            entering = []
            for j in range(per):
                entering.append(state)
                state = dec[:, j * LANES:(j + 1) * LANES] * state + upd[j * dk:(j + 1) * dk, :]
            s_ref[h] = state
            o = o + _dot(q_blk, jnp.concatenate(entering, axis=0).astype(BF16))

            o = o * lax.rsqrt(jnp.mean(o * o, axis=-1, keepdims=True) + LN_EPS)
            o = o * nw_ref[:, h * dv:(h + 1) * dv] * rg_ref[rows, h * dv:(h + 1) * dv].astype(F32)
            o_ref[rows, h * dv:(h + 1) * dv] = o.astype(BF16)
        return 0

    lax.fori_loop(0, n_slabs, slab, 0)


def _gla(q, k, v, la, rg, norm_w, bsz, seq):
    n, kw = q.shape
    vw = v.shape[1]
    blk = lambda w: pl.BlockSpec((seq, w), lambda b: (b, 0))
    return pl.pallas_call(
        _gla_kernel,
        grid=(bsz,),
        in_specs=[blk(kw), blk(kw), blk(vw), blk(kw), blk(vw),
                  pl.BlockSpec((1, vw), lambda b: (0, 0))],
        out_specs=blk(vw),
        out_shape=jax.ShapeDtypeStruct((n, vw), BF16),
        scratch_shapes=[pltpu.VMEM((GLA_HEADS, GLA_KEY_DIM, GLA_VALUE_DIM), F32)],
        compiler_params=_params(("parallel",)),
        name="gla",
    )(q, k, v, la, rg, norm_w)


def _first_index_of_max(v, idx, n):
    m = jnp.max(v, axis=0, keepdims=True)
    first = jnp.min(jnp.where(v == m, idx, float(n)), axis=0, keepdims=True)
    return m, first


def _row_index(n, t):
    return lax.broadcasted_iota(jnp.int32, (n, t), 0).astype(F32)


def _route(scores, bias):
    e, t = scores.shape
    gsz = e // N_GROUPS
    sel = scores + bias
    neg = -jnp.inf
    sub = _row_index(gsz, t)
    gidx = _row_index(N_GROUPS, t)
    gscore = jnp.zeros((N_GROUPS, t), F32)
    for g in range(N_GROUPS):
        v = sel[g * gsz:(g + 1) * gsz, :]
        m1, i1 = _first_index_of_max(v, sub, gsz)
        m2 = jnp.max(jnp.where(sub == i1, neg, v), axis=0, keepdims=True)
        gscore = jnp.where(gidx == float(g), m1 + m2, gscore)
    gmask = jnp.zeros((N_GROUPS, t), F32)
    for _ in range(TOPK_GROUPS):
        _, i1 = _first_index_of_max(gscore, gidx, N_GROUPS)
        hit = gidx == i1
        gmask = jnp.where(hit, 1.0, gmask)
        gscore = jnp.where(hit, neg, gscore)
    emask = jnp.concatenate(
        [jnp.broadcast_to(gmask[g:g + 1, :], (gsz, t)) for g in range(N_GROUPS)], axis=0)
    cand = jnp.where(emask > 0.0, sel, neg)
    eidx = _row_index(e, t)
    w = jnp.zeros((e, t), F32)
    for _ in range(TOP_K):
        _, i1 = _first_index_of_max(cand, eidx, e)
        hit = eidx == i1
        w = jnp.where(hit, scores, w)
        cand = jnp.where(hit, neg, cand)
    return w / jnp.sum(w, axis=0, keepdims=True) * ROUTED_SCALE


def _mix_kernel(osb_ref, og_ref, gsb_ref, ggla_ref, x_ref, mod_ref, wbs_ref, wbg_ref, wout_ref,
                g1_ref, b1_ref, wrh_ref, wrl_ref, rb_ref, x1_ref, h2_ref, gates_ref):
    mod = mod_ref[0]
    gate1, shift2, scale2 = mod[2:3, :], mod[3:4, :], mod[4:5, :]
    y = (gsb_ref[...].astype(F32) * _dot(osb_ref[...], wbs_ref[...])
         + ggla_ref[...].astype(F32) * _dot(og_ref[...], wbg_ref[...]))
    mix = _dot(y.astype(BF16), wout_ref[...])
    x1 = _layer_norm(DN_ALPHA * x_ref[...] + gate1 * mix) * g1_ref[...] + b1_ref[...]
    x1_ref[...] = x1
    h2 = _layer_norm(x1) * (1.0 + scale2) + shift2
    h_hi, h_lo = _split_hi_lo(h2)
    h2_ref[...] = h_hi
    logits_t = _dot_nt(wrh_ref[...], h_hi) + _dot_nt(wrh_ref[...], h_lo) + _dot_nt(wrl_ref[...], h_hi)
    gates_t = _route(jax.nn.sigmoid(logits_t), rb_ref[...])
    tm = gates_t.shape[1]
    padded = jnp.concatenate([gates_t, jnp.zeros((LANES - N_EXPERTS, tm), F32)], axis=0)
    gates_ref[...] = padded.T


def _mix(osb, og, gsb, ggla, x2, mod3, w_bsb, w_bgla, w_out, ln_g, ln_b, wr_hi, wr_lo, rbias, seq, tm):
    n, d = x2.shape
    const = lambda a: pl.BlockSpec(a.shape, lambda i: (0,) * a.ndim, pipeline_mode=pl.Buffered(1))
    row = lambda w: pl.BlockSpec((tm, w), lambda i: (i, 0))
    return pl.pallas_call(
        _mix_kernel,
        grid=(n // tm,),
        in_specs=[row(osb.shape[1]), row(og.shape[1]), row(d), row(d), row(d),
                  pl.BlockSpec((1, 6, d), lambda i: (i * tm // seq, 0, 0)),
                  const(w_bsb), const(w_bgla), const(w_out), const(ln_g), const(ln_b),
                  const(wr_hi), const(wr_lo), const(rbias)],
        out_specs=[row(d), row(d), row(LANES)],
        out_shape=[jax.ShapeDtypeStruct((n, d), F32), jax.ShapeDtypeStruct((n, d), BF16),
                   jax.ShapeDtypeStruct((n, LANES), F32)],
        compiler_params=_params(("parallel",)),
        name="mix",
    )(osb, og, gsb, ggla, x2, mod3, w_bsb, w_bgla, w_out, ln_g, ln_b, wr_hi, wr_lo, rbias)


def _expert(h2, w_gu, w_down, gate_col):
    f = w_down.shape[0]
    gu = _dot(h2, w_gu.astype(BF16))
    act = _silu(gu[:, :f]) * gu[:, f:]
    if gate_col is not None:
        act = act * gate_col
    return _dot(act.astype(BF16), w_down.astype(BF16))


def _moe_kernel(h2_ref, gates_ref, x1_ref, mod_ref, wgu_ref, wd_ref, wsgu_ref, wsd_ref,
                g2_ref, b2_ref, o_ref, acc_ref):
    e = pl.program_id(1)
    n_e = pl.num_programs(1) - 1

    @pl.when(e == 0)
    def _():
        acc_ref[...] = jnp.zeros_like(acc_ref)

    @pl.when(e < n_e)
    def _():
        lane = lax.broadcasted_iota(jnp.int32, (1, LANES), 1)
        gate_col = jnp.sum(jnp.where(lane == e, gates_ref[...], 0.0), axis=1, keepdims=True)
        acc_ref[...] += _expert(h2_ref[...], wgu_ref[0], wd_ref[0], gate_col)

    @pl.when(e == n_e)
    def _():
        ffn = acc_ref[...] + _expert(h2_ref[...], wsgu_ref[...], wsd_ref[...], None)
        gate2 = mod_ref[0][5:6, :]
        o_ref[...] = _layer_norm(DN_ALPHA * x1_ref[...] + gate2 * ffn) * g2_ref[...] + b2_ref[...]


def _moe(h2, gates, x1, mod3, w_gu, w_down, ws_gu, ws_down, ln_g, ln_b, seq, tm):
    n, d = h2.shape
    n_e, _, f2 = w_gu.shape
    f = f2 // 2
    const = lambda a: pl.BlockSpec(a.shape, lambda i, e: (0,) * a.ndim, pipeline_mode=pl.Buffered(1))
    row = lambda w: pl.BlockSpec((tm, w), lambda i, e: (i, 0))
    last = n_e - 1
    return pl.pallas_call(
        _moe_kernel,
        grid=(n // tm, n_e + 1),
        in_specs=[row(d), row(LANES), row(d),
                  pl.BlockSpec((1, 6, d), lambda i, e: (i * tm // seq, 0, 0)),
                  pl.BlockSpec((1, d, f2), lambda i, e: (jnp.minimum(e, last), 0, 0)),
                  pl.BlockSpec((1, f, d), lambda i, e: (jnp.minimum(e, last), 0, 0)),
                  const(ws_gu), const(ws_down), const(ln_g), const(ln_b)],
        out_specs=row(d),
        out_shape=jax.ShapeDtypeStruct((n, d), F32),
        scratch_shapes=[pltpu.VMEM((tm, d), F32)],
        compiler_params=_params(("parallel", "arbitrary")),
        name="moe",
    )(h2, gates, x1, mod3, w_gu, w_down, ws_gu, ws_down, ln_g, ln_b)


def _layer(x, c, w_ada, b_ada, w_in, gla_w_gate_up, gla_b_gate, gla_norm_w, w_branch_sb, w_branch_gla,
           w_out, ln1_g, ln1_b, w_router, router_bias, w_exp_gate_up, w_exp_down, w_shared_gate_up,
           w_shared_down, ln2_g, ln2_b):
    bsz, seq, d = x.shape
    n = bsz * seq
    x2 = x.reshape(n, d)
    mod3 = _ada(c, w_ada, b_ada).reshape(bsz, 6, d)

    sbw = SB_HEADS * SB_HEAD_DIM
    gkw = GLA_HEADS * GLA_KEY_DIM
    gvw = GLA_HEADS * GLA_VALUE_DIM
    n_a = 3 * sbw + 2 * gkw + 2 * gvw
    w_a = w_in[:, :n_a].astype(BF16)
    w_glr = jnp.pad(w_in[:, n_a:n_a + GLA_GATE_RANK], ((0, 0), (0, LANES - GLA_GATE_RANK))).astype(BF16)
    w_gu = jnp.pad(gla_w_gate_up, ((0, LANES - GLA_GATE_RANK), (0, 0))).astype(BF16)
    w_merge = w_in[:, n_a + GLA_GATE_RANK:].astype(BF16)
    tm = min(512, seq)
    (q_sb, k_sb, v_sb, q_g, k_g, v_g, r_g, log_a, gate_sb, gate_gla) = _inproj(
        x2, mod3, w_a, w_glr, w_gu, gla_b_gate.reshape(1, gkw), w_merge, seq, tm)

    o_sb = _sb_attention(q_sb, k_sb, v_sb, bsz, seq, min(MXU_DIM, seq))
    o_g = _gla(q_g, k_g, v_g, log_a, r_g, gla_norm_w.reshape(1, gvw), bsz, seq)

    wr_t = w_router.T
    wr_hi = wr_t.astype(BF16)
    wr_lo = (wr_t - wr_hi.astype(F32)).astype(BF16)
    x1, h2, gates = _mix(o_sb, o_g, gate_sb, gate_gla, x2, mod3,
                         w_branch_sb.astype(BF16), w_branch_gla.astype(BF16), w_out.astype(BF16),
                         ln1_g.reshape(1, d), ln1_b.reshape(1, d), wr_hi, wr_lo,
                         router_bias.reshape(N_EXPERTS, 1), seq, tm)

    out = _moe(h2, gates, x1, mod3, w_exp_gate_up, w_exp_down, w_shared_gate_up, w_shared_down,
               ln2_g.reshape(1, d), ln2_b.reshape(1, d), seq, min(1024, seq))
    return out.reshape(bsz, seq, d)


def kernel(x, c, w_ada, b_ada, w_in, gla_w_gate_up, gla_b_gate, gla_norm_w, w_branch_sb, w_branch_gla,
           w_out, ln1_g, ln1_b, w_router, router_bias, w_exp_gate_up, w_exp_down, w_shared_gate_up,
           w_shared_down, ln2_g, ln2_b):
    depth = w_ada.shape[0]
    for l in range(depth):
        x = _layer(x, c, w_ada[l], b_ada[l], w_in[l], gla_w_gate_up[l], gla_b_gate[l], gla_norm_w[l],
                   w_branch_sb[l], w_branch_gla[l], w_out[l], ln1_g[l], ln1_b[l], w_router[l],
                   router_bias[l], w_exp_gate_up[l], w_exp_down[l], w_shared_gate_up[l],
                   w_shared_down[l], ln2_g[l], ln2_b[l])
    return x
```

```python
import functools
import math

import jax
import jax.numpy as jnp
from jax import lax
from jax.experimental import pallas as pl
from jax.experimental.pallas import tpu as pltpu

F32 = jnp.float32
BF16 = jnp.bfloat16

SB_HEADS = 8
SB_HEAD_DIM = 64
GLA_HEADS = 4
GLA_KEY_DIM = 64
GLA_VALUE_DIM = 128
GLA_CHUNK = 64
GLA_GATE_RANK = 16
GLA_GATE_TEMP = 16.0
N_EXPERTS = 64
TOP_K = 8
N_GROUPS = 8
TOPK_GROUPS = 4
ROUTED_SCALE = 2.5
DEPTH = 1
DN_ALPHA = (2 * DEPTH) ** 0.25
LN_EPS = 1e-5

LANES = 128
MXU_DIM = 256
VMEM_LIMIT = 56 * 1024 * 1024


def _dot(a, b):
    return jnp.dot(a, b, preferred_element_type=F32)


def _dot_nt(a, b):
    return lax.dot_general(a, b, (((1,), (1,)), ((), ())), preferred_element_type=F32)


def _split_hi_lo(x):
    hi = x.astype(BF16)
    lo = (x - hi.astype(F32)).astype(BF16)
    return hi, lo


def _silu(x):
    return x * jax.nn.sigmoid(x)


def _layer_norm(x):
    mu = jnp.mean(x, axis=-1, keepdims=True)
    xc = x - mu
    var = jnp.mean(xc * xc, axis=-1, keepdims=True)
    return xc * lax.rsqrt(var + LN_EPS)


def _params(sem):
    return pltpu.CompilerParams(dimension_semantics=sem, vmem_limit_bytes=VMEM_LIMIT)


def _ada_kernel(c_ref, w_ref, b_ref, o_ref):
    s_hi, s_lo = _split_hi_lo(_silu(c_ref[...]))
    w_hi, w_lo = _split_hi_lo(w_ref[...])
    acc = _dot(s_hi, w_hi) + _dot(s_lo, w_hi) + _dot(s_hi, w_lo)
    o_ref[...] = acc + b_ref[...]


def _ada(c, w_ada, b_ada):
    bsz, d = c.shape
    n_out = w_ada.shape[1]
    tn = d
    return pl.pallas_call(
        _ada_kernel,
        grid=(n_out // tn,),
        in_specs=[pl.BlockSpec((bsz, d), lambda j: (0, 0)),
                  pl.BlockSpec((d, tn), lambda j: (0, j)),
                  pl.BlockSpec((1, tn), lambda j: (0, j))],
        out_specs=pl.BlockSpec((bsz, tn), lambda j: (0, j)),
        out_shape=jax.ShapeDtypeStruct((bsz, n_out), F32),
        compiler_params=_params(("parallel",)),
        name="ada",
    )(c, w_ada, b_ada.reshape(1, n_out))


def _inproj_kernel(x_ref, mod_ref, wa_ref, wglr_ref, wgu_ref, bg_ref, wm_ref,
                   qsb_ref, ksb_ref, vsb_ref, qg_ref, kg_ref, vg_ref, rg_ref, la_ref,
                   gsb_ref, ggla_ref):
    mod = mod_ref[0]
    shift1, scale1 = mod[0:1, :], mod[1:2, :]
    h = _layer_norm(x_ref[...]) * (1.0 + scale1) + shift1
    hb = h.astype(BF16)
    sbw = SB_HEADS * SB_HEAD_DIM
    gkw = GLA_HEADS * GLA_KEY_DIM
    gvw = GLA_HEADS * GLA_VALUE_DIM
    off = 0
    qsb_ref[...] = (_dot(hb, wa_ref[:, off:off + sbw]) * (SB_HEAD_DIM ** -0.5)).astype(BF16); off += sbw
    ksb_ref[...] = _dot(hb, wa_ref[:, off:off + sbw]).astype(BF16); off += sbw
    vsb_ref[...] = _dot(hb, wa_ref[:, off:off + sbw]).astype(BF16); off += sbw
    qg_ref[...] = (_dot(hb, wa_ref[:, off:off + gkw]) * (GLA_KEY_DIM ** -0.5)).astype(BF16); off += gkw
    kg_ref[...] = _dot(hb, wa_ref[:, off:off + gkw]).astype(BF16); off += gkw
    vg_ref[...] = _dot(hb, wa_ref[:, off:off + gvw]).astype(BF16); off += gvw
    rg_ref[...] = _silu(_dot(hb, wa_ref[:, off:off + gvw])).astype(BF16)
    g_lr = _dot(hb, wglr_ref[...])
    z = _dot(g_lr.astype(BF16), wgu_ref[...]) + bg_ref[...]
    la_ref[...] = (jnp.minimum(z, 0.0) - jnp.log(1.0 + jnp.exp(-jnp.abs(z)))) * (1.0 / GLA_GATE_TEMP)
    d = gsb_ref.shape[1]
    gsb_ref[...] = jax.nn.sigmoid(_dot(hb, wm_ref[:, 0:d])).astype(BF16)
    ggla_ref[...] = jax.nn.sigmoid(_dot(hb, wm_ref[:, d:2 * d])).astype(BF16)


def _inproj(x2, mod3, w_a, w_glr, w_gu, b_gate, w_merge, seq, tm):
    n, d = x2.shape
    sbw = SB_HEADS * SB_HEAD_DIM
    gkw = GLA_HEADS * GLA_KEY_DIM
    gvw = GLA_HEADS * GLA_VALUE_DIM
    const = lambda shape: pl.BlockSpec(shape, lambda i: (0,) * len(shape), pipeline_mode=pl.Buffered(1))
    row = lambda w: pl.BlockSpec((tm, w), lambda i: (i, 0))
    widths = (sbw, sbw, sbw, gkw, gkw, gvw, gvw)
    out_shape = [jax.ShapeDtypeStruct((n, w), BF16) for w in widths]
    out_shape += [jax.ShapeDtypeStruct((n, gkw), F32),
                  jax.ShapeDtypeStruct((n, d), BF16), jax.ShapeDtypeStruct((n, d), BF16)]
    out_specs = [row(w) for w in widths] + [row(gkw), row(d), row(d)]
    return pl.pallas_call(
        _inproj_kernel,
        grid=(n // tm,),
        in_specs=[row(d),
                  pl.BlockSpec((1, 6, d), lambda i: (i * tm // seq, 0, 0)),
                  const(w_a.shape), const(w_glr.shape), const(w_gu.shape), const(b_gate.shape),
                  const(w_merge.shape)],
        out_specs=out_specs,
        out_shape=out_shape,
        compiler_params=_params(("parallel",)),
        name="inproj",
    )(x2, mod3, w_a, w_glr, w_gu, b_gate, w_merge)


def _sb_kernel(q_ref, k_ref, v_ref, o_ref, z_ref, t_ref, rest_ref, acc_ref, carry_ref, *, tq):
    qi = pl.program_id(2)
    q = q_ref[...]
    lane = lax.broadcasted_iota(jnp.int32, (1, LANES), 1)
    head0 = lane < SB_HEAD_DIM
    zero = jnp.zeros_like(q)
    q2 = jnp.concatenate([jnp.where(head0, q, zero), jnp.where(head0, zero, q)], axis=0)
    r = lax.broadcasted_iota(jnp.int32, (2 * tq, tq), 0)
    c = lax.broadcasted_iota(jnp.int32, (2 * tq, tq), 1)
    upper = ((r % tq) > c).astype(BF16)

    def rows_of(j):
        return pl.ds(pl.multiple_of(jnp.maximum(j, 0) * tq, tq), tq)

    def scores(j):
        z_ref[...] = _dot_nt(q2, k_ref[rows_of(j), :])

    def log_terms(diag):
        z = z_ref[...]
        ls_neg = jnp.minimum(-z, 0.0) - jnp.log(1.0 + jnp.exp(-jnp.abs(z)))
        t = ls_neg + z + carry_ref[...]
        if diag:
            strict = c < (r % tq)
            ls_neg = jnp.where(strict, ls_neg, 0.0)
            t = jnp.where(strict, t, -jnp.inf)
        hi, lo = _split_hi_lo(ls_neg)
        rest_ref[...] = _dot(jnp.concatenate([hi, lo], axis=1), upper)
        t_ref[...] = t
        carry_ref[...] += jnp.sum(ls_neg, axis=1, keepdims=True)

    def weigh(j):
        w = jnp.exp(t_ref[...] + rest_ref[...])
        acc_ref[...] += _dot(w.astype(BF16), v_ref[rows_of(j), :])

    acc_ref[...] = jnp.zeros_like(acc_ref)
    carry_ref[...] = jnp.zeros_like(carry_ref)
    scores(qi)
    log_terms(True)
    scores(qi - 1)

    def body(i, _):
        weigh(qi - i)
        log_terms(False)
        scores(qi - 2 - i)
        return 0

    lax.fori_loop(0, qi, body, 0)
    weigh(0)
    acc = acc_ref[...]
    o_ref[...] = jnp.where(head0, acc[:tq], acc[tq:]).astype(BF16)


def _sb_attention(q, k, v, bsz, seq, tq):
    n, w = q.shape
    n_pairs = w // LANES
    nq = seq // tq
    return pl.pallas_call(
        functools.partial(_sb_kernel, tq=tq),
        grid=(bsz, n_pairs, nq),
        in_specs=[pl.BlockSpec((tq, LANES), lambda b, p, i: (b * nq + i, p)),
                  pl.BlockSpec((seq, LANES), lambda b, p, i: (b, p)),
                  pl.BlockSpec((seq, LANES), lambda b, p, i: (b, p))],
        out_specs=pl.BlockSpec((tq, LANES), lambda b, p, i: (b * nq + i, p)),
        out_shape=jax.ShapeDtypeStruct((n, w), BF16),
        scratch_shapes=[pltpu.VMEM((2 * tq, tq), F32), pltpu.VMEM((2 * tq, tq), F32),
                        pltpu.VMEM((2 * tq, tq), F32), pltpu.VMEM((2 * tq, LANES), F32),
                        pltpu.VMEM((2 * tq, 1), F32)],
        compiler_params=_params(("parallel", "parallel", "arbitrary")),
        name="sb_attn",
    )(q, k, v)


GLA_SLAB = MXU_DIM
GLA_SUB = 16


def _gla_constants():
    s, c, sb = GLA_SLAB, GLA_CHUNK, GLA_SUB
    t = lax.broadcasted_iota(jnp.int32, (s, s), 0)
    u = lax.broadcasted_iota(jnp.int32, (s, s), 1)
    same_chunk = (t // c) == (u // c)

    def cum_to(row):
        return (same_chunk & (u <= row)).astype(F32)

    g = cum_to(t)
    g_total = same_chunk.astype(F32)
    sub_start = (t // sb) * sb
    next_sub_start = jnp.minimum(sub_start + sb, (t // c) * c + c - 1)
    chunk_mid = (t // c) * c + c // 2
    mats = (g,
            g_total - g,
            g - cum_to(sub_start),
            cum_to(next_sub_start) - g,
            g - cum_to(chunk_mid))
    return [m.astype(BF16) for m in mats]


def _gla_kernel(q_ref, k_ref, v_ref, la_ref, rg_ref, nw_ref, o_ref, s_ref):
    s, c, sb = GLA_SLAB, GLA_CHUNK, GLA_SUB
    dk, dv, nh = GLA_KEY_DIM, GLA_VALUE_DIM, GLA_HEADS
    kw = nh * dk
    n_slabs = q_ref.shape[0] // s
    per = s // c
    m_g, m_kdec, m_q0, m_k1, m_q2 = _gla_constants()

    t = lax.broadcasted_iota(jnp.int32, (s, s), 0)
    u = lax.broadcasted_iota(jnp.int32, (s, s), 1)
    causal = ((t // c) == (u // c)) & (u <= t)
    bt, bu = (t % c) // sb, (u % c) // sb
    lvl0 = causal & (bt == bu)
    lvl1 = causal & ((bt % 2) == 1) & (bu == bt - 1)
    lvl2 = causal & (bt >= 2) & (bu <= 1)
    chunk_of_row_is_lane_block = (t // c) == (u // dk)
    lane_block_is_chunk_of_col = (t // dk) == (u // c)
    lane_kw = lax.broadcasted_iota(jnp.int32, (1, kw), 1)
    rowi = lax.broadcasted_iota(jnp.int32, (kw, kw), 0)
    coli = lax.broadcasted_iota(jnp.int32, (kw, kw), 1)

    s_ref[...] = jnp.zeros_like(s_ref)

    def slab(i, _):
        r0 = pl.multiple_of(i * s, s)
        rows = pl.ds(r0, s)
        la_hi, la_lo = _split_hi_lo(la_ref[rows, :])

        def cum(m):
            return _dot(m, la_hi) + _dot(m, la_lo)

        q = q_ref[rows, :].astype(F32)
        k = k_ref[rows, :].astype(F32)
        e_q0 = cum(m_q0)
        e_q2 = cum(m_q2)
        qg = (q * jnp.exp(cum(m_g))).astype(BF16)
        kdec = (k * jnp.exp(cum(m_kdec))).astype(BF16)
        q0 = q * jnp.exp(e_q0)
        k0 = (k * jnp.exp(-e_q0)).astype(BF16)
        k1 = (k * jnp.exp(cum(m_k1))).astype(BF16)
        q2 = q * jnp.exp(e_q2)
        k2 = (k * jnp.exp(-e_q2)).astype(BF16)
        g_last = cum(m_g + m_kdec)
        gl_rows = jnp.concatenate(
            [jnp.broadcast_to(g_last[j * c:j * c + 1, :], (LANES, kw)) for j in range(per)], axis=0)
        gl_hi, gl_lo = _split_hi_lo(gl_rows)

        for h in range(nh):
            in_head = (lane_kw >= h * dk) & (lane_kw < (h + 1) * dk)
            sel = ((rowi // dk == h) & (rowi % dk == coli % dk)).astype(BF16)
            sel_t = ((coli // dk == h) & (coli % dk == rowi % dk)).astype(BF16)
            v = v_ref[rows, h * dv:(h + 1) * dv]

            a0 = _dot_nt(jnp.where(in_head, q0, 0.0).astype(BF16), k0)
            a1 = _dot_nt(jnp.where(in_head, q0, 0.0).astype(BF16), k1)
            a2 = _dot_nt(jnp.where(in_head, q2, 0.0).astype(BF16), k2)
            a = jnp.where(lvl0, a0, jnp.where(lvl1, a1, jnp.where(lvl2, a2, 0.0)))
            o = _dot(a.astype(BF16), v)

            q_blk = jnp.where(chunk_of_row_is_lane_block, _dot(qg, sel), 0.0).astype(BF16)
            k_blk_t = jnp.where(lane_block_is_chunk_of_col, _dot_nt(sel_t, kdec), 0.0).astype(BF16)
            upd = _dot(k_blk_t, v)
            dec = jnp.exp(_dot_nt(sel_t[0:dk, :], gl_hi) + _dot_nt(sel_t[0:dk, :], gl_lo))
            state = s_ref[h]
            entering = []
            for j in range(per):
                entering.append(state)
                state = dec[:, j * LANES:(j + 1) * LANES] * state + upd[j * dk:(j + 1) * dk, :]
            s_ref[h] = state
            o = o + _dot(q_blk, jnp.concatenate(entering, axis=0).astype(BF16))

            o = o * lax.rsqrt(jnp.mean(o * o, axis=-1, keepdims=True) + LN_EPS)
            o = o * nw_ref[:, h * dv:(h + 1) * dv] * rg_ref[rows, h * dv:(h + 1) * dv].astype(F32)
            o_ref[rows, h * dv:(h + 1) * dv] = o.astype(BF16)
        return 0

    lax.fori_loop(0, n_slabs, slab, 0)


def _gla(q, k, v, la, rg, norm_w, bsz, seq):
    n, kw = q.shape
    vw = v.shape[1]
    blk = lambda w: pl.BlockSpec((seq, w), lambda b: (b, 0))
    return pl.pallas_call(
        _gla_kernel,
        grid=(bsz,),
        in_specs=[blk(kw), blk(kw), blk(vw), blk(kw), blk(vw),
                  pl.BlockSpec((1, vw), lambda b: (0, 0))],
        out_specs=blk(vw),
        out_shape=jax.ShapeDtypeStruct((n, vw), BF16),
        scratch_shapes=[pltpu.VMEM((GLA_HEADS, GLA_KEY_DIM, GLA_VALUE_DIM), F32)],
        compiler_params=_params(("parallel",)),
        name="gla",
    )(q, k, v, la, rg, norm_w)


def _first_index_of_max(v, idx, n):
    m = jnp.max(v, axis=0, keepdims=True)
    first = jnp.min(jnp.where(v == m, idx, float(n)), axis=0, keepdims=True)
    return m, first


def _row_index(n, t):
    return lax.broadcasted_iota(jnp.int32, (n, t), 0).astype(F32)


def _route(scores, bias):
    e, t = scores.shape
    gsz = e // N_GROUPS
    sel = scores + bias
    neg = -jnp.inf
    sub = _row_index(gsz, t)
    gidx = _row_index(N_GROUPS, t)
    gscore = jnp.zeros((N_GROUPS, t), F32)
    for g in range(N_GROUPS):
        v = sel[g * gsz:(g + 1) * gsz, :]
        m1, i1 = _first_index_of_max(v, sub, gsz)
        m2 = jnp.max(jnp.where(sub == i1, neg, v), axis=0, keepdims=True)
        gscore = jnp.where(gidx == float(g), m1 + m2, gscore)
    gmask = jnp.zeros((N_GROUPS, t), F32)
    for _ in range(TOPK_GROUPS):
        _, i1 = _first_index_of_max(gscore, gidx, N_GROUPS)
        hit = gidx == i1
        gmask = jnp.where(hit, 1.0, gmask)
        gscore = jnp.where(hit, neg, gscore)
    emask = jnp.concatenate(
        [jnp.broadcast_to(gmask[g:g + 1, :], (gsz, t)) for g in range(N_GROUPS)], axis=0)
    cand = jnp.where(emask > 0.0, sel, neg)
    eidx = _row_index(e, t)
    w = jnp.zeros((e, t), F32)
    for _ in range(TOP_K):
        _, i1 = _first_index_of_max(cand, eidx, e)
        hit = eidx == i1
        w = jnp.where(hit, scores, w)
        cand = jnp.where(hit, neg, cand)
    return w / jnp.sum(w, axis=0, keepdims=True) * ROUTED_SCALE


def _mix_kernel(osb_ref, og_ref, gsb_ref, ggla_ref, x_ref, mod_ref, wbs_ref, wbg_ref, wout_ref,
                g1_ref, b1_ref, wrh_ref, wrl_ref, rb_ref, x1_ref, h2_ref, gates_ref):
    mod = mod_ref[0]
    gate1, shift2, scale2 = mod[2:3, :], mod[3:4, :], mod[4:5, :]
    y = (gsb_ref[...].astype(F32) * _dot(osb_ref[...], wbs_ref[...])
         + ggla_ref[...].astype(F32) * _dot(og_ref[...], wbg_ref[...]))
    mix = _dot(y.astype(BF16), wout_ref[...])
    x1 = _layer_norm(DN_ALPHA * x_ref[...] + gate1 * mix) * g1_ref[...] + b1_ref[...]
    x1_ref[...] = x1
    h2 = _layer_norm(x1) * (1.0 + scale2) + shift2
    h_hi, h_lo = _split_hi_lo(h2)
    h2_ref[...] = h_hi
    logits_t = _dot_nt(wrh_ref[...], h_hi) + _dot_nt(wrh_ref[...], h_lo) + _dot_nt(wrl_ref[...], h_hi)
    gates_t = _route(jax.nn.sigmoid(logits_t), rb_ref[...])
    tm = gates_t.shape[1]
    padded = jnp.concatenate([gates_t, jnp.zeros((LANES - N_EXPERTS, tm), F32)], axis=0)
    gates_ref[...] = padded.T


def _mix(osb, og, gsb, ggla, x2, mod3, w_bsb, w_bgla, w_out, ln_g, ln_b, wr_hi, wr_lo, rbias, seq, tm):
    n, d = x2.shape
    const = lambda a: pl.BlockSpec(a.shape, lambda i: (0,) * a.ndim, pipeline_mode=pl.Buffered(1))
    row = lambda w: pl.BlockSpec((tm, w), lambda i: (i, 0))
    return pl.pallas_call(
        _mix_kernel,
        grid=(n // tm,),
        in_specs=[row(osb.shape[1]), row(og.shape[1]), row(d), row(d), row(d),
                  pl.BlockSpec((1, 6, d), lambda i: (i * tm // seq, 0, 0)),
                  const(w_bsb), const(w_bgla), const(w_out), const(ln_g), const(ln_b),
                  const(wr_hi), const(wr_lo), const(rbias)],
        out_specs=[row(d), row(d), row(LANES)],
        out_shape=[jax.ShapeDtypeStruct((n, d), F32), jax.ShapeDtypeStruct((n, d), BF16),
                   jax.ShapeDtypeStruct((n, LANES), F32)],
        compiler_params=_params(("parallel",)),
        name="mix",
    )(osb, og, gsb, ggla, x2, mod3, w_bsb, w_bgla, w_out, ln_g, ln_b, wr_hi, wr_lo, rbias)


def _expert(h2, w_gu, w_down, gate_col):
    f = w_down.shape[0]
    gu = _dot(h2, w_gu.astype(BF16))
    act = _silu(gu[:, :f]) * gu[:, f:]
    if gate_col is not None:
        act = act * gate_col
    return _dot(act.astype(BF16), w_down.astype(BF16))


def _moe_kernel(h2_ref, gates_ref, x1_ref, mod_ref, wgu_ref, wd_ref, wsgu_ref, wsd_ref,
                g2_ref, b2_ref, o_ref, acc_ref):
    e = pl.program_id(1)
    n_e = pl.num_programs(1) - 1

    @pl.when(e == 0)
    def _():
        acc_ref[...] = jnp.zeros_like(acc_ref)

    @pl.when(e < n_e)
    def _():
        lane = lax.broadcasted_iota(jnp.int32, (1, LANES), 1)
        gate_col = jnp.sum(jnp.where(lane == e, gates_ref[...], 0.0), axis=1, keepdims=True)
        acc_ref[...] += _expert(h2_ref[...], wgu_ref[0], wd_ref[0], gate_col)

    @pl.when(e == n_e)
    def _():
        ffn = acc_ref[...] + _expert(h2_ref[...], wsgu_ref[...], wsd_ref[...], None)
        gate2 = mod_ref[0][5:6, :]
        o_ref[...] = _layer_norm(DN_ALPHA * x1_ref[...] + gate2 * ffn) * g2_ref[...] + b2_ref[...]


def _moe(h2, gates, x1, mod3, w_gu, w_down, ws_gu, ws_down, ln_g, ln_b, seq, tm):
    n, d = h2.shape
    n_e, _, f2 = w_gu.shape
    f = f2 // 2
    const = lambda a: pl.BlockSpec(a.shape, lambda i, e: (0,) * a.ndim, pipeline_mode=pl.Buffered(1))
    row = lambda w: pl.BlockSpec((tm, w), lambda i, e: (i, 0))
    last = n_e - 1
    return pl.pallas_call(
        _moe_kernel,
        grid=(n // tm, n_e + 1),
        in_specs=[row(d), row(LANES), row(d),
                  pl.BlockSpec((1, 6, d), lambda i, e: (i * tm // seq, 0, 0)),
                  pl.BlockSpec((1, d, f2), lambda i, e: (jnp.minimum(e, last), 0, 0)),
                  pl.BlockSpec((1, f, d), lambda i, e: (jnp.minimum(e, last), 0, 0)),
                  const(ws_gu), const(ws_down), const(ln_g), const(ln_b)],
        out_specs=row(d),
        out_shape=jax.ShapeDtypeStruct((n, d), F32),
        scratch_shapes=[pltpu.VMEM((tm, d), F32)],
        compiler_params=_params(("parallel", "arbitrary")),
        name="moe",
    )(h2, gates, x1, mod3, w_gu, w_down, ws_gu, ws_down, ln_g, ln_b)


def _layer(x, c, w_ada, b_ada, w_in, gla_w_gate_up, gla_b_gate, gla_norm_w, w_branch_sb, w_branch_gla,
           w_out, ln1_g, ln1_b, w_router, router_bias, w_exp_gate_up, w_exp_down, w_shared_gate_up,
           w_shared_down, ln2_g, ln2_b):
    bsz, seq, d = x.shape
    n = bsz * seq
    x2 = x.reshape(n, d)
    mod3 = _ada(c, w_ada, b_ada).reshape(bsz, 6, d)

    sbw = SB_HEADS * SB_HEAD_DIM
    gkw = GLA_HEADS * GLA_KEY_DIM
    gvw = GLA_HEADS * GLA_VALUE_DIM
    n_a = 3 * sbw + 2 * gkw + 2 * gvw
    w_a = w_in[:, :n_a].astype(BF16)
    w_glr = jnp.pad(w_in[:, n_a:n_a + GLA_GATE_RANK], ((0, 0), (0, LANES - GLA_GATE_RANK))).astype(BF16)
    w_gu = jnp.pad(gla_w_gate_up, ((0, LANES - GLA_GATE_RANK), (0, 0))).astype(BF16)
    w_merge = w_in[:, n_a + GLA_GATE_RANK:].astype(BF16)
    tm = min(512, seq)
    (q_sb, k_sb, v_sb, q_g, k_g, v_g, r_g, log_a, gate_sb, gate_gla) = _inproj(
        x2, mod3, w_a, w_glr, w_gu, gla_b_gate.reshape(1, gkw), w_merge, seq, tm)

    o_sb = _sb_attention(q_sb, k_sb, v_sb, bsz, seq, min(MXU_DIM, seq))
    o_g = _gla(q_g, k_g, v_g, log_a, r_g, gla_norm_w.reshape(1, gvw), bsz, seq)

    wr_t = w_router.T
    wr_hi = wr_t.astype(BF16)
    wr_lo = (wr_t - wr_hi.astype(F32)).astype(BF16)
    x1, h2, gates = _mix(o_sb, o_g, gate_sb, gate_gla, x2, mod3,
                         w_branch_sb.astype(BF16), w_branch_gla.astype(BF16), w_out.astype(BF16),
                         ln1_g.reshape(1, d), ln1_b.reshape(1, d), wr_hi, wr_lo,
                         router_bias.reshape(N_EXPERTS, 1), seq, tm)

    out = _moe(h2, gates, x1, mod3, w_exp_gate_up, w_exp_down, w_shared_gate_up, w_shared_down,
               ln2_g.reshape(1, d), ln2_b.reshape(1, d), seq, min(1024, seq))
    return out.reshape(bsz, seq, d)


def kernel(x, c, w_ada, b_ada, w_in, gla_w_gate_up, gla_b_gate, gla_norm_w, w_branch_sb, w_branch_gla,
           w_out, ln1_g, ln1_b, w_router, router_bias, w_exp_gate_up, w_exp_down, w_shared_gate_up,
           w_shared_down, ln2_g, ln2_b):
    depth = w_ada.shape[0]
    for l in range(depth):
        x = _layer(x, c, w_ada[l], b_ada[l], w_in[l], gla_w_gate_up[l], gla_b_gate[l], gla_norm_w[l],
                   w_branch_sb[l], w_branch_gla[l], w_out[l], ln1_g[l], ln1_b[l], w_router[l],
                   router_bias[l], w_exp_gate_up[l], w_exp_down[l], w_shared_gate_up[l],
                   w_shared_down[l], ln2_g[l], ln2_b[l])
    return x
```

```python
import functools
import math

import jax
import jax.numpy as jnp
from jax import lax
from jax.experimental import pallas as pl
from jax.experimental.pallas import tpu as pltpu

F32 = jnp.float32
BF16 = jnp.bfloat16

SB_HEADS = 8
SB_HEAD_DIM = 64
GLA_HEADS = 4
GLA_KEY_DIM = 64
GLA_VALUE_DIM = 128
GLA_CHUNK = 64
GLA_GATE_RANK = 16
GLA_GATE_TEMP = 16.0
N_EXPERTS = 64
TOP_K = 8
N_GROUPS = 8
TOPK_GROUPS = 4
ROUTED_SCALE = 2.5
DEPTH = 1
DN_ALPHA = (2 * DEPTH) ** 0.25
LN_EPS = 1e-5

LANES = 128
MXU_DIM = 256
VMEM_LIMIT = 56 * 1024 * 1024


def _dot(a, b):
    return jnp.dot(a, b, preferred_element_type=F32)


def _dot_nt(a, b):
    return lax.dot_general(a, b, (((1,), (1,)), ((), ())), preferred_element_type=F32)


def _split_hi_lo(x):
    hi = x.astype(BF16)
    lo = (x - hi.astype(F32)).astype(BF16)
    return hi, lo


def _silu(x):
    return x * jax.nn.sigmoid(x)


def _layer_norm(x):
    mu = jnp.mean(x, axis=-1, keepdims=True)
    xc = x - mu
    var = jnp.mean(xc * xc, axis=-1, keepdims=True)
    return xc * lax.rsqrt(var + LN_EPS)


def _params(sem):
    return pltpu.CompilerParams(dimension_semantics=sem, vmem_limit_bytes=VMEM_LIMIT)


def _ada_kernel(c_ref, w_ref, b_ref, o_ref):
    s_hi, s_lo = _split_hi_lo(_silu(c_ref[...]))
    w_hi, w_lo = _split_hi_lo(w_ref[...])
    acc = _dot(s_hi, w_hi) + _dot(s_lo, w_hi) + _dot(s_hi, w_lo)
    o_ref[...] = acc + b_ref[...]


def _ada(c, w_ada, b_ada):
    bsz, d = c.shape
    n_out = w_ada.shape[1]
    tn = d
    return pl.pallas_call(
        _ada_kernel,
        grid=(n_out // tn,),
        in_specs=[pl.BlockSpec((bsz, d), lambda j: (0, 0)),
                  pl.BlockSpec((d, tn), lambda j: (0, j)),
                  pl.BlockSpec((1, tn), lambda j: (0, j))],
        out_specs=pl.BlockSpec((bsz, tn), lambda j: (0, j)),
        out_shape=jax.ShapeDtypeStruct((bsz, n_out), F32),
        compiler_params=_params(("parallel",)),
        name="ada",
    )(c, w_ada, b_ada.reshape(1, n_out))


def _inproj_kernel(x_ref, mod_ref, wa_ref, wglr_ref, wgu_ref, bg_ref, wm_ref,
                   qsb_ref, ksb_ref, vsb_ref, qg_ref, kg_ref, vg_ref, rg_ref, la_ref,
                   gsb_ref, ggla_ref):
    mod = mod_ref[0]
    shift1, scale1 = mod[0:1, :], mod[1:2, :]
    h = _layer_norm(x_ref[...]) * (1.0 + scale1) + shift1
    hb = h.astype(BF16)
    sbw = SB_HEADS * SB_HEAD_DIM
    gkw = GLA_HEADS * GLA_KEY_DIM
    gvw = GLA_HEADS * GLA_VALUE_DIM
    off = 0
    qsb_ref[...] = (_dot(hb, wa_ref[:, off:off + sbw]) * (SB_HEAD_DIM ** -0.5)).astype(BF16); off += sbw
    ksb_ref[...] = _dot(hb, wa_ref[:, off:off + sbw]).astype(BF16); off += sbw
    vsb_ref[...] = _dot(hb, wa_ref[:, off:off + sbw]).astype(BF16); off += sbw
    qg_ref[...] = (_dot(hb, wa_ref[:, off:off + gkw]) * (GLA_KEY_DIM ** -0.5)).astype(BF16); off += gkw
    kg_ref[...] = _dot(hb, wa_ref[:, off:off + gkw]).astype(BF16); off += gkw
    vg_ref[...] = _dot(hb, wa_ref[:, off:off + gvw]).astype(BF16); off += gvw
    rg_ref[...] = _silu(_dot(hb, wa_ref[:, off:off + gvw])).astype(BF16)
    g_lr = _dot(hb, wglr_ref[...])
    z = _dot(g_lr.astype(BF16), wgu_ref[...]) + bg_ref[...]
    la_ref[...] = (jnp.minimum(z, 0.0) - jnp.log(1.0 + jnp.exp(-jnp.abs(z)))) * (1.0 / GLA_GATE_TEMP)
    d = gsb_ref.shape[1]
    gsb_ref[...] = jax.nn.sigmoid(_dot(hb, wm_ref[:, 0:d])).astype(BF16)
    ggla_ref[...] = jax.nn.sigmoid(_dot(hb, wm_ref[:, d:2 * d])).astype(BF16)


def _inproj(x2, mod3, w_a, w_glr, w_gu, b_gate, w_merge, seq, tm):
    n, d = x2.shape
    sbw = SB_HEADS * SB_HEAD_DIM
    gkw = GLA_HEADS * GLA_KEY_DIM
    gvw = GLA_HEADS * GLA_VALUE_DIM
    const = lambda shape: pl.BlockSpec(shape, lambda i: (0,) * len(shape), pipeline_mode=pl.Buffered(1))
    row = lambda w: pl.BlockSpec((tm, w), lambda i: (i, 0))
    widths = (sbw, sbw, sbw, gkw, gkw, gvw, gvw)
    out_shape = [jax.ShapeDtypeStruct((n, w), BF16) for w in widths]
    out_shape += [jax.ShapeDtypeStruct((n, gkw), F32),
                  jax.ShapeDtypeStruct((n, d), BF16), jax.ShapeDtypeStruct((n, d), BF16)]
    out_specs = [row(w) for w in widths] + [row(gkw), row(d), row(d)]
    return pl.pallas_call(
        _inproj_kernel,
        grid=(n // tm,),
        in_specs=[row(d),
                  pl.BlockSpec((1, 6, d), lambda i: (i * tm // seq, 0, 0)),
                  const(w_a.shape), const(w_glr.shape), const(w_gu.shape), const(b_gate.shape),
                  const(w_merge.shape)],
        out_specs=out_specs,
        out_shape=out_shape,
        compiler_params=_params(("parallel",)),
        name="inproj",
    )(x2, mod3, w_a, w_glr, w_gu, b_gate, w_merge)


def _sb_kernel(q_ref, k_ref, v_ref, o_ref, z_ref, t_ref, rest_ref, acc_ref, carry_ref, *, tq):
    qi = pl.program_id(2)
    q = q_ref[...]
    lane = lax.broadcasted_iota(jnp.int32, (1, LANES), 1)
    head0 = lane < SB_HEAD_DIM
    zero = jnp.zeros_like(q)
    q2 = jnp.concatenate([jnp.where(head0, q, zero), jnp.where(head0, zero, q)], axis=0)
    r = lax.broadcasted_iota(jnp.int32, (2 * tq, tq), 0)
    c = lax.broadcasted_iota(jnp.int32, (2 * tq, tq), 1)
    upper = ((r % tq) > c).astype(BF16)

    def rows_of(j):
        return pl.ds(pl.multiple_of(jnp.maximum(j, 0) * tq, tq), tq)

    def scores(j):
        z_ref[...] = _dot_nt(q2, k_ref[rows_of(j), :])

    def log_terms(diag):
        z = z_ref[...]
        ls_neg = jnp.minimum(-z, 0.0) - jnp.log(1.0 + jnp.exp(-jnp.abs(z)))
        t = ls_neg + z + carry_ref[...]
        if diag:
            strict = c < (r % tq)
            ls_neg = jnp.where(strict, ls_neg, 0.0)
            t = jnp.where(strict, t, -jnp.inf)
        hi, lo = _split_hi_lo(ls_neg)
        rest_ref[...] = _dot(jnp.concatenate([hi, lo], axis=1), upper)
        t_ref[...] = t
        carry_ref[...] += jnp.sum(ls_neg, axis=1, keepdims=True)

    def weigh(j):
        w = jnp.exp(t_ref[...] + rest_ref[...])
        acc_ref[...] += _dot(w.astype(BF16), v_ref[rows_of(j), :])

    acc_ref[...] = jnp.zeros_like(acc_ref)
    carry_ref[...] = jnp.zeros_like(carry_ref)
    scores(qi)
    log_terms(True)
    scores(qi - 1)

    def body(i, _):
        weigh(qi - i)
        log_terms(False)
        scores(qi - 2 - i)
        return 0

    lax.fori_loop(0, qi, body, 0)
    weigh(0)
    acc = acc_ref[...]
    o_ref[...] = jnp.where(head0, acc[:tq], acc[tq:]).astype(BF16)


def _sb_attention(q, k, v, bsz, seq, tq):
    n, w = q.shape
    n_pairs = w // LANES
    nq = seq // tq
    return pl.pallas_call(
        functools.partial(_sb_kernel, tq=tq),
        grid=(bsz, n_pairs, nq),
        in_specs=[pl.BlockSpec((tq, LANES), lambda b, p, i: (b * nq + i, p)),
                  pl.BlockSpec((seq, LANES), lambda b, p, i: (b, p)),
                  pl.BlockSpec((seq, LANES), lambda b, p, i: (b, p))],
        out_specs=pl.BlockSpec((tq, LANES), lambda b, p, i: (b * nq + i, p)),
        out_shape=jax.ShapeDtypeStruct((n, w), BF16),
        scratch_shapes=[pltpu.VMEM((2 * tq, tq), F32), pltpu.VMEM((2 * tq, tq), F32),
                        pltpu.VMEM((2 * tq, tq), F32), pltpu.VMEM((2 * tq, LANES), F32),
                        pltpu.VMEM((2 * tq, 1), F32)],
        compiler_params=_params(("parallel", "parallel", "arbitrary")),
        name="sb_attn",
    )(q, k, v)


GLA_SLAB = MXU_DIM
GLA_SUB = 16


def _gla_constants():
    s, c, sb = GLA_SLAB, GLA_CHUNK, GLA_SUB
    t = lax.broadcasted_iota(jnp.int32, (s, s), 0)
    u = lax.broadcasted_iota(jnp.int32, (s, s), 1)
    same_chunk = (t // c) == (u // c)

    def cum_to(row):
        return (same_chunk & (u <= row)).astype(F32)

    g = cum_to(t)
    g_total = same_chunk.astype(F32)
    sub_start = (t // sb) * sb
    next_sub_start = jnp.minimum(sub_start + sb, (t // c) * c + c - 1)
    chunk_mid = (t // c) * c + c // 2
    mats = (g,
            g_total - g,
            g - cum_to(sub_start),
            cum_to(next_sub_start) - g,
            g - cum_to(chunk_mid))
    return [m.astype(BF16) for m in mats]


def _gla_kernel(q_ref, k_ref, v_ref, la_ref, rg_ref, nw_ref, o_ref, s_ref):
    s, c, sb = GLA_SLAB, GLA_CHUNK, GLA_SUB
    dk, dv, nh = GLA_KEY_DIM, GLA_VALUE_DIM, GLA_HEADS
    kw = nh * dk
    n_slabs = q_ref.shape[0] // s
    per = s // c
    m_g, m_kdec, m_q0, m_k1, m_q2 = _gla_constants()

    t = lax.broadcasted_iota(jnp.int32, (s, s), 0)
    u = lax.broadcasted_iota(jnp.int32, (s, s), 1)
    causal = ((t // c) == (u // c)) & (u <= t)
    bt, bu = (t % c) // sb, (u % c) // sb
    lvl0 = causal & (bt == bu)
    lvl1 = causal & ((bt % 2) == 1) & (bu == bt - 1)
    lvl2 = causal & (bt >= 2) & (bu <= 1)
    chunk_of_row_is_lane_block = (t // c) == (u // dk)
    lane_block_is_chunk_of_col = (t // dk) == (u // c)
    lane_kw = lax.broadcasted_iota(jnp.int32, (1, kw), 1)
    rowi = lax.broadcasted_iota(jnp.int32, (kw, kw), 0)
    coli = lax.broadcasted_iota(jnp.int32, (kw, kw), 1)

    s_ref[...] = jnp.zeros_like(s_ref)

    def slab(i, _):
        r0 = pl.multiple_of(i * s, s)
        rows = pl.ds(r0, s)
        la_hi, la_lo = _split_hi_lo(la_ref[rows, :])

        def cum(m):
            return _dot(m, la_hi) + _dot(m, la_lo)

        q = q_ref[rows, :].astype(F32)
        k = k_ref[rows, :].astype(F32)
        e_q0 = cum(m_q0)
        e_q2 = cum(m_q2)
        qg = (q * jnp.exp(cum(m_g))).astype(BF16)
        kdec = (k * jnp.exp(cum(m_kdec))).astype(BF16)
        q0 = q * jnp.exp(e_q0)
        k0 = (k * jnp.exp(-e_q0)).astype(BF16)
        k1 = (k * jnp.exp(cum(m_k1))).astype(BF16)
        q2 = q * jnp.exp(e_q2)
        k2 = (k * jnp.exp(-e_q2)).astype(BF16)
        g_last = cum(m_g + m_kdec)
        gl_rows = jnp.concatenate(
            [jnp.broadcast_to(g_last[j * c:j * c + 1, :], (LANES, kw)) for j in range(per)], axis=0)
        gl_hi, gl_lo = _split_hi_lo(gl_rows)

        for h in range(nh):
            in_head = (lane_kw >= h * dk) & (lane_kw < (h + 1) * dk)
            sel = ((rowi // dk == h) & (rowi % dk == coli % dk)).astype(BF16)
            sel_t = ((coli // dk == h) & (coli % dk == rowi % dk)).astype(BF16)
            v = v_ref[rows, h * dv:(h + 1) * dv]

            a0 = _dot_nt(jnp.where(in_head, q0, 0.0).astype(BF16), k0)
            a1 = _dot_nt(jnp.where(in_head, q0, 0.0).astype(BF16), k1)
            a2 = _dot_nt(jnp.where(in_head, q2, 0.0).astype(BF16), k2)
            a = jnp.where(lvl0, a0, jnp.where(lvl1, a1, jnp.where(lvl2, a2, 0.0)))
            o = _dot(a.astype(BF16), v)

            q_blk = jnp.where(chunk_of_row_is_lane_block, _dot(qg, sel), 0.0).astype(BF16)
            k_blk_t = jnp.where(lane_block_is_chunk_of_col, _dot_nt(sel_t, kdec), 0.0).astype(BF16)
            upd = _dot(k_blk_t, v)
            dec = jnp.exp(_dot_nt(sel_t[0:dk, :], gl_hi) + _dot_nt(sel_t[0:dk, :], gl_lo))
            state = s_ref[h]
            entering = []
            for j in range(per):
                entering.append(state)
                state = dec[:, j * LANES:(j + 1) * LANES] * state + upd[j * dk:(j + 1) * dk, :]
            s_ref[h] = state
            o = o + _dot(q_blk, jnp.concatenate(entering, axis=0).astype(BF16))

            o = o * lax.rsqrt(jnp.mean(o * o, axis=-1, keepdims=True) + LN_EPS)
            o = o * nw_ref[:, h * dv:(h + 1) * dv] * rg_ref[rows, h * dv:(h + 1) * dv].astype(F32)
            o_ref[rows, h * dv:(h + 1) * dv] = o.astype(BF16)
        return 0

    lax.fori_loop(0, n_slabs, slab, 0)


def _gla(q, k, v, la, rg, norm_w, bsz, seq):
    n, kw = q.shape
    vw = v.shape[1]
    blk = lambda w: pl.BlockSpec((seq, w), lambda b: (b, 0))
    return pl.pallas_call(
        _gla_kernel,
        grid=(bsz,),
        in_specs=[blk(kw), blk(kw), blk(vw), blk(kw), blk(vw),
                  pl.BlockSpec((1, vw), lambda b: (0, 0))],
        out_specs=blk(vw),
        out_shape=jax.ShapeDtypeStruct((n, vw), BF16),
        scratch_shapes=[pltpu.VMEM((GLA_HEADS, GLA_KEY_DIM, GLA_VALUE_DIM), F32)],
        compiler_params=_params(("parallel",)),
        name="gla",
    )(q, k, v, la, rg, norm_w)


def _first_index_of_max(v, idx, n):
    m = jnp.max(v, axis=0, keepdims=True)
    first = jnp.min(jnp.where(v == m, idx, float(n)), axis=0, keepdims=True)
    return m, first


def _row_index(n, t):
    return lax.broadcasted_iota(jnp.int32, (n, t), 0).astype(F32)


def _route(scores, bias):
    e, t = scores.shape
    gsz = e // N_GROUPS
    sel = scores + bias
    neg = -jnp.inf
    sub = _row_index(gsz, t)
    gidx = _row_index(N_GROUPS, t)
    gscore = jnp.zeros((N_GROUPS, t), F32)
    for g in range(N_GROUPS):
        v = sel[g * gsz:(g + 1) * gsz, :]
        m1, i1 = _first_index_of_max(v, sub, gsz)
        m2 = jnp.max(jnp.where(sub == i1, neg, v), axis=0, keepdims=True)
        gscore = jnp.where(gidx == float(g), m1 + m2, gscore)
    gmask = jnp.zeros((N_GROUPS, t), F32)
    for _ in range(TOPK_GROUPS):
        _, i1 = _first_index_of_max(gscore, gidx, N_GROUPS)
        hit = gidx == i1
        gmask = jnp.where(hit, 1.0, gmask)
        gscore = jnp.where(hit, neg, gscore)
    emask = jnp.concatenate(
        [jnp.broadcast_to(gmask[g:g + 1, :], (gsz, t)) for g in range(N_GROUPS)], axis=0)
    cand = jnp.where(emask > 0.0, sel, neg)
    eidx = _row_index(e, t)
    kidx = _row_index(TOP_K, t)
    ids = jnp.zeros((TOP_K, t), F32)
    w = jnp.zeros((TOP_K, t), F32)
    for k in range(TOP_K):
        _, i1 = _first_index_of_max(cand, eidx, e)
        hit = eidx == i1
        ids = jnp.where(kidx == float(k), i1, ids)
        w = jnp.where(kidx == float(k), jnp.sum(jnp.where(hit, scores, 0.0), axis=0, keepdims=True), w)
        cand = jnp.where(hit, neg, cand)
    return ids, w / jnp.sum(w, axis=0, keepdims=True) * ROUTED_SCALE


def _mix_kernel(osb_ref, og_ref, gsb_ref, ggla_ref, x_ref, mod_ref, wbs_ref, wbg_ref, wout_ref,
                g1_ref, b1_ref, wrh_ref, wrl_ref, rb_ref, x1_ref, h2_ref, h2row_ref, ids_ref, gw_ref):
    mod = mod_ref[0]
    gate1, shift2, scale2 = mod[2:3, :], mod[3:4, :], mod[4:5, :]
    y = (gsb_ref[...].astype(F32) * _dot(osb_ref[...], wbs_ref[...])
         + ggla_ref[...].astype(F32) * _dot(og_ref[...], wbg_ref[...]))
    mix = _dot(y.astype(BF16), wout_ref[...])
    x1 = _layer_norm(DN_ALPHA * x_ref[...] + gate1 * mix) * g1_ref[...] + b1_ref[...]
    x1_ref[...] = x1
    h2 = _layer_norm(x1) * (1.0 + scale2) + shift2
    h_hi, h_lo = _split_hi_lo(h2)
    h2_ref[...] = h_hi
    h2row_ref[...] = h_hi.astype(F32).reshape(h2row_ref.shape)
    logits_t = _dot_nt(wrh_ref[...], h_hi) + _dot_nt(wrh_ref[...], h_lo) + _dot_nt(wrl_ref[...], h_hi)
    ids, gw = _route(jax.nn.sigmoid(logits_t), rb_ref[...])
    ids_ref[...] = ids.astype(jnp.int32)
    gw_ref[...] = gw


def _mix(osb, og, gsb, ggla, x2, mod3, w_bsb, w_bgla, w_out, ln_g, ln_b, wr_hi, wr_lo, rbias, seq, tm):
    n, d = x2.shape
    const = lambda a: pl.BlockSpec(a.shape, lambda i: (0,) * a.ndim, pipeline_mode=pl.Buffered(1))
    row = lambda w: pl.BlockSpec((tm, w), lambda i: (i, 0))
    col = pl.BlockSpec((TOP_K, tm), lambda i: (0, i))
    return pl.pallas_call(
        _mix_kernel,
        grid=(n // tm,),
        in_specs=[row(osb.shape[1]), row(og.shape[1]), row(d), row(d), row(d),
                  pl.BlockSpec((1, 6, d), lambda i: (i * tm // seq, 0, 0)),
                  const(w_bsb), const(w_bgla), const(w_out), const(ln_g), const(ln_b),
                  const(wr_hi), const(wr_lo), const(rbias)],
        out_specs=[row(d), row(d), pl.BlockSpec((tm, 1, d), lambda i: (i, 0, 0)), col, col],
        out_shape=[jax.ShapeDtypeStruct((n, d), F32), jax.ShapeDtypeStruct((n, d), BF16),
                   jax.ShapeDtypeStruct((n, 1, d), F32),
                   jax.ShapeDtypeStruct((TOP_K, n), jnp.int32), jax.ShapeDtypeStruct((TOP_K, n), F32)],
        compiler_params=_params(("parallel",)),
        name="mix",
    )(osb, og, gsb, ggla, x2, mod3, w_bsb, w_bgla, w_out, ln_g, ln_b, wr_hi, wr_lo, rbias)


MOE_TM = MXU_DIM
MOE_CHUNK = 4096
MOE_TOKEN_BITS = 12
MOE_GATE_SHIFT = 32 - MOE_TOKEN_BITS - 9


def _dispatch_tables(ids_t, gw_t, ct, tm):
    k, n = ids_t.shape
    nc = n // ct
    ids = ids_t.T.reshape(nc, ct, k)
    gw = gw_t.T.reshape(nc, ct, k)
    member = (ids[..., None] == jnp.arange(N_EXPERTS, dtype=jnp.int32)).astype(jnp.int32).sum(2)
    seen = jnp.cumsum(member, axis=1)
    cnt = seen[:, -1, :]
    padded = (cnt + tm - 1) // tm * tm
    off = jnp.cumsum(padded, axis=1) - padded
    rank = jnp.take_along_axis(seen, ids, axis=2) - 1
    dest = jnp.take_along_axis(jnp.broadcast_to(off[:, None, :], seen.shape), ids, axis=2) + rank
    gbits = lax.bitcast_convert_type(gw, jnp.uint32)
    gbits = (gbits + (1 << (MOE_GATE_SHIFT - 1))) >> MOE_GATE_SHIFT
    token = jnp.arange(ct, dtype=jnp.uint32)[None, :, None]
    entry = lax.bitcast_convert_type((gbits << MOE_TOKEN_BITS) | token, jnp.int32)
    rows = ct * k + N_EXPERTS * tm
    chunk = jnp.broadcast_to(jnp.arange(nc)[:, None, None], dest.shape)
    table = jnp.zeros((nc, rows), jnp.int32).at[chunk, dest].set(entry, unique_indices=True)
    return table.reshape(nc, 1, rows), off.reshape(-1).astype(jnp.int32), cnt.reshape(-1).astype(jnp.int32)


def _moe_kernel(off_ref, cnt_ref, tab_ref, h_ref, wgu_ref, wd_ref, out_ref,
                xrow_ref, x2_ref, yrow_ref, wgub_ref, wdb_ref):
    c, e, n_e = pl.program_id(0), pl.program_id(1), pl.num_programs(1)
    tm, _, d = xrow_ref.shape
    f = wd_ref.shape[1]

    @pl.when(e == 0)
    def _():
        out_ref[...] = jnp.zeros_like(out_ref)

    wgub_ref[...] = wgu_ref[0].astype(BF16)
    wdb_ref[...] = wd_ref[0].astype(BF16)
    n_rows = cnt_ref[c * n_e + e]
    first = off_ref[c * n_e + e]

    def entry(i):
        p = tab_ref[0, 0, i]
        token = p & ((1 << MOE_TOKEN_BITS) - 1)
        gate_bits = lax.shift_left(lax.shift_right_logical(p, MOE_TOKEN_BITS), MOE_GATE_SHIFT)
        return token, gate_bits

    def tile(i, _):
        base = first + i * tm

        def gather(j, _):
            r0 = pl.multiple_of(j * 8, 8)
            for u in range(8):
                token, _ = entry(base + r0 + u)
                xrow_ref[r0 + u] = h_ref[token]
            return 0

        lax.fori_loop(0, tm // 8, gather, 0)
        x2_ref[...] = xrow_ref[...].reshape(tm, d)
        gu = _dot(x2_ref[...].astype(BF16), wgub_ref[...])
        act = _silu(gu[:, :f]) * gu[:, f:]
        yrow_ref[...] = _dot(act.astype(BF16), wdb_ref[...]).reshape(tm, 1, d)

        def add_rows(r0, count):
            ents = [entry(base + r0 + u) for u in range(count)]
            olds = [out_ref[token] for token, _ in ents]
            for u, (token, gate_bits) in enumerate(ents):
                gate = lax.bitcast_convert_type(jnp.full((1, d), gate_bits, jnp.int32), F32)
                out_ref[token] = olds[u] + gate * yrow_ref[r0 + u]

        valid = jnp.minimum(n_rows - i * tm, tm)

        def add8(j, _):
            add_rows(pl.multiple_of(j * 8, 8), 8)
            return 0

        def add1(r, _):
            add_rows(r, 1)
            return 0

        lax.fori_loop(0, valid // 8, add8, 0)
        lax.fori_loop(valid // 8 * 8, valid, add1, 0)
        return 0

    lax.fori_loop(0, (n_rows + tm - 1) // tm, tile, 0)


def _moe(h2row, table, off, cnt, w_gu, w_down, ct, tm):
    n, _, d = h2row.shape
    n_e, _, f2 = w_gu.shape
    f = f2 // 2
    rows = table.shape[2]
    grid_spec = pltpu.PrefetchScalarGridSpec(
        num_scalar_prefetch=2,
        grid=(n // ct, n_e),
        in_specs=[pl.BlockSpec((1, 1, rows), lambda c, e, *_: (c, 0, 0), memory_space=pltpu.SMEM),
                  pl.BlockSpec((ct, 1, d), lambda c, e, *_: (c, 0, 0), pipeline_mode=pl.Buffered(1)),
                  pl.BlockSpec((1, d, f2), lambda c, e, *_: (e, 0, 0)),
                  pl.BlockSpec((1, f, d), lambda c, e, *_: (e, 0, 0))],
        out_specs=pl.BlockSpec((ct, 1, d), lambda c, e, *_: (c, 0, 0), pipeline_mode=pl.Buffered(1)),
        scratch_shapes=[pltpu.VMEM((tm, 1, d), F32), pltpu.VMEM((tm, d), F32), pltpu.VMEM((tm, 1, d), F32),
                        pltpu.VMEM((d, f2), BF16), pltpu.VMEM((f, d), BF16)])
    return pl.pallas_call(
        _moe_kernel,
        grid_spec=grid_spec,
        out_shape=jax.ShapeDtypeStruct((n, 1, d), F32),
        compiler_params=_params(("arbitrary", "arbitrary")),
        name="moe",
    )(off, cnt, table, h2row, w_gu, w_down)


def _post_kernel(routed_ref, h2_ref, x1_ref, mod_ref, wsgu_ref, wsd_ref, g2_ref, b2_ref, o_ref, r2_ref):
    tm, d = r2_ref.shape
    f = wsd_ref.shape[0]
    r2_ref[...] = routed_ref[...].reshape(tm, d)
    gu = _dot(h2_ref[...], wsgu_ref[...])
    shared = _dot((_silu(gu[:, :f]) * gu[:, f:]).astype(BF16), wsd_ref[...])
    gate2 = mod_ref[0][5:6, :]
    ffn = r2_ref[...] + shared
    o_ref[...] = _layer_norm(DN_ALPHA * x1_ref[...] + gate2 * ffn) * g2_ref[...] + b2_ref[...]


def _post(routed, h2, x1, mod3, ws_gu, ws_down, ln_g, ln_b, seq, tm):
    n, d = h2.shape
    const = lambda a: pl.BlockSpec(a.shape, lambda i: (0,) * a.ndim, pipeline_mode=pl.Buffered(1))
    row = lambda w: pl.BlockSpec((tm, w), lambda i: (i, 0))
    return pl.pallas_call(
        _post_kernel,
        grid=(n // tm,),
        in_specs=[pl.BlockSpec((tm, 1, d), lambda i: (i, 0, 0)), row(d), row(d),
                  pl.BlockSpec((1, 6, d), lambda i: (i * tm // seq, 0, 0)),
                  const(ws_gu), const(ws_down), const(ln_g), const(ln_b)],
        out_specs=row(d),
        out_shape=jax.ShapeDtypeStruct((n, d), F32),
        scratch_shapes=[pltpu.VMEM((tm, d), F32)],
        compiler_params=_params(("parallel",)),
        name="post",
    )(routed, h2, x1, mod3, ws_gu, ws_down, ln_g, ln_b)


def _layer(x, c, w_ada, b_ada, w_in, gla_w_gate_up, gla_b_gate, gla_norm_w, w_branch_sb, w_branch_gla,
           w_out, ln1_g, ln1_b, w_router, router_bias, w_exp_gate_up, w_exp_down, w_shared_gate_up,
           w_shared_down, ln2_g, ln2_b):
    bsz, seq, d = x.shape
    n = bsz * seq
    x2 = x.reshape(n, d)
    mod3 = _ada(c, w_ada, b_ada).reshape(bsz, 6, d)

    sbw = SB_HEADS * SB_HEAD_DIM
    gkw = GLA_HEADS * GLA_KEY_DIM
    gvw = GLA_HEADS * GLA_VALUE_DIM
    n_a = 3 * sbw + 2 * gkw + 2 * gvw
    w_a = w_in[:, :n_a].astype(BF16)
    w_glr = jnp.pad(w_in[:, n_a:n_a + GLA_GATE_RANK], ((0, 0), (0, LANES - GLA_GATE_RANK))).astype(BF16)
    w_gu = jnp.pad(gla_w_gate_up, ((0, LANES - GLA_GATE_RANK), (0, 0))).astype(BF16)
    w_merge = w_in[:, n_a + GLA_GATE_RANK:].astype(BF16)
    tm = min(512, seq)
    (q_sb, k_sb, v_sb, q_g, k_g, v_g, r_g, log_a, gate_sb, gate_gla) = _inproj(
        x2, mod3, w_a, w_glr, w_gu, gla_b_gate.reshape(1, gkw), w_merge, seq, tm)

    o_sb = _sb_attention(q_sb, k_sb, v_sb, bsz, seq, min(MXU_DIM, seq))
    o_g = _gla(q_g, k_g, v_g, log_a, r_g, gla_norm_w.reshape(1, gvw), bsz, seq)

    wr_t = w_router.T
    wr_hi = wr_t.astype(BF16)
    wr_lo = (wr_t - wr_hi.astype(F32)).astype(BF16)
    x1, h2, h2row, ids_t, gw_t = _mix(o_sb, o_g, gate_sb, gate_gla, x2, mod3,
                                      w_branch_sb.astype(BF16), w_branch_gla.astype(BF16), w_out.astype(BF16),
                                      ln1_g.reshape(1, d), ln1_b.reshape(1, d), wr_hi, wr_lo,
                                      router_bias.reshape(N_EXPERTS, 1), seq, tm)

    ct = min(MOE_CHUNK, n)
    table, off, cnt = _dispatch_tables(ids_t, gw_t, ct, MOE_TM)
    routed = _moe(h2row, table, off, cnt, w_exp_gate_up, w_exp_down, ct, MOE_TM)
    out = _post(routed, h2, x1, mod3, w_shared_gate_up.astype(BF16), w_shared_down.astype(BF16),
                ln2_g.reshape(1, d), ln2_b.reshape(1, d), seq, tm)
    return out.reshape(bsz, seq, d)


def kernel(x, c, w_ada, b_ada, w_in, gla_w_gate_up, gla_b_gate, gla_norm_w, w_branch_sb, w_branch_gla,
           w_out, ln1_g, ln1_b, w_router, router_bias, w_exp_gate_up, w_exp_down, w_shared_gate_up,
           w_shared_down, ln2_g, ln2_b):
    depth = w_ada.shape[0]
    for l in range(depth):
        x = _layer(x, c, w_ada[l], b_ada[l], w_in[l], gla_w_gate_up[l], gla_b_gate[l], gla_norm_w[l],
                   w_branch_sb[l], w_branch_gla[l], w_out[l], ln1_g[l], ln1_b[l], w_router[l],
                   router_bias[l], w_exp_gate_up[l], w_exp_down[l], w_shared_gate_up[l],
                   w_shared_down[l], ln2_g[l], ln2_b[l])
    return x
```

```python
import functools
import math

import jax
import jax.numpy as jnp
from jax import lax
from jax.experimental import pallas as pl
from jax.experimental.pallas import tpu as pltpu

F32 = jnp.float32
BF16 = jnp.bfloat16

SB_HEADS = 8
SB_HEAD_DIM = 64
GLA_HEADS = 4
GLA_KEY_DIM = 64
GLA_VALUE_DIM = 128
GLA_CHUNK = 64
GLA_GATE_RANK = 16
GLA_GATE_TEMP = 16.0
N_EXPERTS = 64
TOP_K = 8
N_GROUPS = 8
TOPK_GROUPS = 4
ROUTED_SCALE = 2.5
DEPTH = 1
DN_ALPHA = (2 * DEPTH) ** 0.25
LN_EPS = 1e-5

LANES = 128
MXU_DIM = 256
VMEM_LIMIT = 56 * 1024 * 1024


def _dot(a, b):
    return jnp.dot(a, b, preferred_element_type=F32)


def _dot_nt(a, b):
    return lax.dot_general(a, b, (((1,), (1,)), ((), ())), preferred_element_type=F32)


def _split_hi_lo(x):
    hi = x.astype(BF16)
    lo = (x - hi.astype(F32)).astype(BF16)
    return hi, lo


def _silu(x):
    return x * jax.nn.sigmoid(x)


def _layer_norm(x):
    mu = jnp.mean(x, axis=-1, keepdims=True)
    xc = x - mu
    var = jnp.mean(xc * xc, axis=-1, keepdims=True)
    return xc * lax.rsqrt(var + LN_EPS)


SUBLANES = 8


def _store_row_groups(ref, x):
    m = x.shape[0]
    for j in range(SUBLANES):
        ref[pl.ds(j, m, stride=SUBLANES), :] = x[:, j * LANES:(j + 1) * LANES]


def _load_row_groups(ref):
    m = ref.shape[0] // SUBLANES
    return jnp.concatenate([ref[pl.ds(j, m, stride=SUBLANES), :] for j in range(SUBLANES)], axis=1)


def _params(sem):
    return pltpu.CompilerParams(dimension_semantics=sem, vmem_limit_bytes=VMEM_LIMIT)


def _ada_kernel(c_ref, w_ref, b_ref, o_ref):
    s_hi, s_lo = _split_hi_lo(_silu(c_ref[...]))
    w_hi, w_lo = _split_hi_lo(w_ref[...])
    acc = _dot(s_hi, w_hi) + _dot(s_lo, w_hi) + _dot(s_hi, w_lo)
    o_ref[...] = acc + b_ref[...]


def _ada(c, w_ada, b_ada):
    bsz, d = c.shape
    n_out = w_ada.shape[1]
    tn = d
    return pl.pallas_call(
        _ada_kernel,
        grid=(n_out // tn,),
        in_specs=[pl.BlockSpec((bsz, d), lambda j: (0, 0)),
                  pl.BlockSpec((d, tn), lambda j: (0, j)),
                  pl.BlockSpec((1, tn), lambda j: (0, j))],
        out_specs=pl.BlockSpec((bsz, tn), lambda j: (0, j)),
        out_shape=jax.ShapeDtypeStruct((bsz, n_out), F32),
        compiler_params=_params(("parallel",)),
        name="ada",
    )(c, w_ada, b_ada.reshape(1, n_out))


def _inproj_kernel(x_ref, mod_ref, wa_ref, wglr_ref, wgu_ref, bg_ref, wm_ref,
                   qsb_ref, ksb_ref, vsb_ref, qg_ref, kg_ref, vg_ref, rg_ref, la_ref,
                   gsb_ref, ggla_ref):
    mod = mod_ref[0]
    shift1, scale1 = mod[0:1, :], mod[1:2, :]
    h = _layer_norm(x_ref[...]) * (1.0 + scale1) + shift1
    hb = h.astype(BF16)
    sbw = SB_HEADS * SB_HEAD_DIM
    gkw = GLA_HEADS * GLA_KEY_DIM
    gvw = GLA_HEADS * GLA_VALUE_DIM
    off = 0
    qsb_ref[...] = (_dot(hb, wa_ref[:, off:off + sbw]) * (SB_HEAD_DIM ** -0.5)).astype(BF16); off += sbw
    ksb_ref[...] = _dot(hb, wa_ref[:, off:off + sbw]).astype(BF16); off += sbw
    vsb_ref[...] = _dot(hb, wa_ref[:, off:off + sbw]).astype(BF16); off += sbw
    qg_ref[...] = (_dot(hb, wa_ref[:, off:off + gkw]) * (GLA_KEY_DIM ** -0.5)).astype(BF16); off += gkw
    kg_ref[...] = _dot(hb, wa_ref[:, off:off + gkw]).astype(BF16); off += gkw
    vg_ref[...] = _dot(hb, wa_ref[:, off:off + gvw]).astype(BF16); off += gvw
    rg_ref[...] = _silu(_dot(hb, wa_ref[:, off:off + gvw])).astype(BF16)
    g_lr = _dot(hb, wglr_ref[...])
    z = _dot(g_lr.astype(BF16), wgu_ref[...]) + bg_ref[...]
    la_ref[...] = (jnp.minimum(z, 0.0) - jnp.log(1.0 + jnp.exp(-jnp.abs(z)))) * (1.0 / GLA_GATE_TEMP)
    d = gsb_ref.shape[1]
    gsb_ref[...] = jax.nn.sigmoid(_dot(hb, wm_ref[:, 0:d])).astype(BF16)
    ggla_ref[...] = jax.nn.sigmoid(_dot(hb, wm_ref[:, d:2 * d])).astype(BF16)


def _inproj(x2, mod3, w_a, w_glr, w_gu, b_gate, w_merge, seq, tm):
    n, d = x2.shape
    sbw = SB_HEADS * SB_HEAD_DIM
    gkw = GLA_HEADS * GLA_KEY_DIM
    gvw = GLA_HEADS * GLA_VALUE_DIM
    const = lambda shape: pl.BlockSpec(shape, lambda i: (0,) * len(shape), pipeline_mode=pl.Buffered(1))
    row = lambda w: pl.BlockSpec((tm, w), lambda i: (i, 0))
    widths = (sbw, sbw, sbw, gkw, gkw, gvw, gvw)
    out_shape = [jax.ShapeDtypeStruct((n, w), BF16) for w in widths]
    out_shape += [jax.ShapeDtypeStruct((n, gkw), F32),
                  jax.ShapeDtypeStruct((n, d), BF16), jax.ShapeDtypeStruct((n, d), BF16)]
    out_specs = [row(w) for w in widths] + [row(gkw), row(d), row(d)]
    return pl.pallas_call(
        _inproj_kernel,
        grid=(n // tm,),
        in_specs=[row(d),
                  pl.BlockSpec((1, 6, d), lambda i: (i * tm // seq, 0, 0)),
                  const(w_a.shape), const(w_glr.shape), const(w_gu.shape), const(b_gate.shape),
                  const(w_merge.shape)],
        out_specs=out_specs,
        out_shape=out_shape,
        compiler_params=_params(("parallel",)),
        name="inproj",
    )(x2, mod3, w_a, w_glr, w_gu, b_gate, w_merge)


def _sb_kernel(q_ref, k_ref, v_ref, o_ref, z_ref, t_ref, rest_ref, acc_ref, carry_ref, *, tq):
    qi = pl.program_id(2)
    q = q_ref[...]
    lane = lax.broadcasted_iota(jnp.int32, (1, LANES), 1)
    head0 = lane < SB_HEAD_DIM
    zero = jnp.zeros_like(q)
    q2 = jnp.concatenate([jnp.where(head0, q, zero), jnp.where(head0, zero, q)], axis=0)
    r = lax.broadcasted_iota(jnp.int32, (2 * tq, tq), 0)
    c = lax.broadcasted_iota(jnp.int32, (2 * tq, tq), 1)
    upper = ((r % tq) > c).astype(BF16)

    def rows_of(j):
        return pl.ds(pl.multiple_of(jnp.maximum(j, 0) * tq, tq), tq)

    def scores(j):
        z_ref[...] = _dot_nt(q2, k_ref[rows_of(j), :])

    def log_terms(diag):
        z = z_ref[...]
        ls_neg = jnp.minimum(-z, 0.0) - jnp.log(1.0 + jnp.exp(-jnp.abs(z)))
        t = ls_neg + z + carry_ref[...]
        if diag:
            strict = c < (r % tq)
            ls_neg = jnp.where(strict, ls_neg, 0.0)
            t = jnp.where(strict, t, -jnp.inf)
        hi, lo = _split_hi_lo(ls_neg)
        rest_ref[...] = _dot(jnp.concatenate([hi, lo], axis=1), upper)
        t_ref[...] = t
        carry_ref[...] += jnp.sum(ls_neg, axis=1, keepdims=True)

    def weigh(j):
        w = jnp.exp(t_ref[...] + rest_ref[...])
        acc_ref[...] += _dot(w.astype(BF16), v_ref[rows_of(j), :])

    acc_ref[...] = jnp.zeros_like(acc_ref)
    carry_ref[...] = jnp.zeros_like(carry_ref)
    scores(qi)
    log_terms(True)
    scores(qi - 1)

    def body(i, _):
        weigh(qi - i)
        log_terms(False)
        scores(qi - 2 - i)
        return 0

    lax.fori_loop(0, qi, body, 0)
    weigh(0)
    acc = acc_ref[...]
    o_ref[...] = jnp.where(head0, acc[:tq], acc[tq:]).astype(BF16)


def _sb_attention(q, k, v, bsz, seq, tq):
    n, w = q.shape
    n_pairs = w // LANES
    nq = seq // tq
    return pl.pallas_call(
        functools.partial(_sb_kernel, tq=tq),
        grid=(bsz, n_pairs, nq),
        in_specs=[pl.BlockSpec((tq, LANES), lambda b, p, i: (b * nq + i, p)),
                  pl.BlockSpec((seq, LANES), lambda b, p, i: (b, p)),
                  pl.BlockSpec((seq, LANES), lambda b, p, i: (b, p))],
        out_specs=pl.BlockSpec((tq, LANES), lambda b, p, i: (b * nq + i, p)),
        out_shape=jax.ShapeDtypeStruct((n, w), BF16),
        scratch_shapes=[pltpu.VMEM((2 * tq, tq), F32), pltpu.VMEM((2 * tq, tq), F32),
                        pltpu.VMEM((2 * tq, tq), F32), pltpu.VMEM((2 * tq, LANES), F32),
                        pltpu.VMEM((2 * tq, 1), F32)],
        compiler_params=_params(("parallel", "parallel", "arbitrary")),
        name="sb_attn",
    )(q, k, v)


GLA_SLAB = MXU_DIM
GLA_SUB = 16


def _gla_constants():
    s, c, sb = GLA_SLAB, GLA_CHUNK, GLA_SUB
    t = lax.broadcasted_iota(jnp.int32, (s, s), 0)
    u = lax.broadcasted_iota(jnp.int32, (s, s), 1)
    same_chunk = (t // c) == (u // c)

    def cum_to(row):
        return (same_chunk & (u <= row)).astype(F32)

    g = cum_to(t)
    g_total = same_chunk.astype(F32)
    sub_start = (t // sb) * sb
    next_sub_start = jnp.minimum(sub_start + sb, (t // c) * c + c - 1)
    chunk_mid = (t // c) * c + c // 2
    mats = (g,
            g_total - g,
            g - cum_to(sub_start),
            cum_to(next_sub_start) - g,
            g - cum_to(chunk_mid))
    return [m.astype(BF16) for m in mats]


def _gla_kernel(q_ref, k_ref, v_ref, la_ref, rg_ref, nw_ref, o_ref, s_ref):
    s, c, sb = GLA_SLAB, GLA_CHUNK, GLA_SUB
    dk, dv, nh = GLA_KEY_DIM, GLA_VALUE_DIM, GLA_HEADS
    kw = nh * dk
    n_slabs = q_ref.shape[0] // s
    per = s // c
    m_g, m_kdec, m_q0, m_k1, m_q2 = _gla_constants()

    t = lax.broadcasted_iota(jnp.int32, (s, s), 0)
    u = lax.broadcasted_iota(jnp.int32, (s, s), 1)
    causal = ((t // c) == (u // c)) & (u <= t)
    bt, bu = (t % c) // sb, (u % c) // sb
    lvl0 = causal & (bt == bu)
    lvl1 = causal & ((bt % 2) == 1) & (bu == bt - 1)
    lvl2 = causal & (bt >= 2) & (bu <= 1)
    chunk_of_row_is_lane_block = (t // c) == (u // dk)
    lane_block_is_chunk_of_col = (t // dk) == (u // c)
    lane_kw = lax.broadcasted_iota(jnp.int32, (1, kw), 1)
    rowi = lax.broadcasted_iota(jnp.int32, (kw, kw), 0)
    coli = lax.broadcasted_iota(jnp.int32, (kw, kw), 1)

    s_ref[...] = jnp.zeros_like(s_ref)

    def slab(i, _):
        r0 = pl.multiple_of(i * s, s)
        rows = pl.ds(r0, s)
        la_hi, la_lo = _split_hi_lo(la_ref[rows, :])

        def cum(m):
            return _dot(m, la_hi) + _dot(m, la_lo)

        q = q_ref[rows, :].astype(F32)
        k = k_ref[rows, :].astype(F32)
        e_q0 = cum(m_q0)
        e_q2 = cum(m_q2)
        qg = (q * jnp.exp(cum(m_g))).astype(BF16)
        kdec = (k * jnp.exp(cum(m_kdec))).astype(BF16)
        q0 = q * jnp.exp(e_q0)
        k0 = (k * jnp.exp(-e_q0)).astype(BF16)
        k1 = (k * jnp.exp(cum(m_k1))).astype(BF16)
        q2 = q * jnp.exp(e_q2)
        k2 = (k * jnp.exp(-e_q2)).astype(BF16)
        g_last = cum(m_g + m_kdec)
        gl_rows = jnp.concatenate(
            [jnp.broadcast_to(g_last[j * c:j * c + 1, :], (LANES, kw)) for j in range(per)], axis=0)
        gl_hi, gl_lo = _split_hi_lo(gl_rows)

        for h in range(nh):
            in_head = (lane_kw >= h * dk) & (lane_kw < (h + 1) * dk)
            sel = ((rowi // dk == h) & (rowi % dk == coli % dk)).astype(BF16)
            sel_t = ((coli // dk == h) & (coli % dk == rowi % dk)).astype(BF16)
            v = v_ref[rows, h * dv:(h + 1) * dv]

            a0 = _dot_nt(jnp.where(in_head, q0, 0.0).astype(BF16), k0)
            a1 = _dot_nt(jnp.where(in_head, q0, 0.0).astype(BF16), k1)
            a2 = _dot_nt(jnp.where(in_head, q2, 0.0).astype(BF16), k2)
            a = jnp.where(lvl0, a0, jnp.where(lvl1, a1, jnp.where(lvl2, a2, 0.0)))
            o = _dot(a.astype(BF16), v)

            q_blk = jnp.where(chunk_of_row_is_lane_block, _dot(qg, sel), 0.0).astype(BF16)
            k_blk_t = jnp.where(lane_block_is_chunk_of_col, _dot_nt(sel_t, kdec), 0.0).astype(BF16)
            upd = _dot(k_blk_t, v)
            dec = jnp.exp(_dot_nt(sel_t[0:dk, :], gl_hi) + _dot_nt(sel_t[0:dk, :], gl_lo))
            state = s_ref[h]
            entering = []
            for j in range(per):
                entering.append(state)
                state = dec[:, j * LANES:(j + 1) * LANES] * state + upd[j * dk:(j + 1) * dk, :]
            s_ref[h] = state
            o = o + _dot(q_blk, jnp.concatenate(entering, axis=0).astype(BF16))

            o = o * lax.rsqrt(jnp.mean(o * o, axis=-1, keepdims=True) + LN_EPS)
            o = o * nw_ref[:, h * dv:(h + 1) * dv] * rg_ref[rows, h * dv:(h + 1) * dv].astype(F32)
            o_ref[rows, h * dv:(h + 1) * dv] = o.astype(BF16)
        return 0

    lax.fori_loop(0, n_slabs, slab, 0)


def _gla(q, k, v, la, rg, norm_w, bsz, seq):
    n, kw = q.shape
    vw = v.shape[1]
    blk = lambda w: pl.BlockSpec((seq, w), lambda b: (b, 0))
    return pl.pallas_call(
        _gla_kernel,
        grid=(bsz,),
        in_specs=[blk(kw), blk(kw), blk(vw), blk(kw), blk(vw),
                  pl.BlockSpec((1, vw), lambda b: (0, 0))],
        out_specs=blk(vw),
        out_shape=jax.ShapeDtypeStruct((n, vw), BF16),
        scratch_shapes=[pltpu.VMEM((GLA_HEADS, GLA_KEY_DIM, GLA_VALUE_DIM), F32)],
        compiler_params=_params(("parallel",)),
        name="gla",
    )(q, k, v, la, rg, norm_w)


def _first_index_of_max(v, idx, n):
    m = jnp.max(v, axis=0, keepdims=True)
    first = jnp.min(jnp.where(v == m, idx, float(n)), axis=0, keepdims=True)
    return m, first


def _row_index(n, t):
    return lax.broadcasted_iota(jnp.int32, (n, t), 0).astype(F32)


def _route(scores, bias):
    e, t = scores.shape
    gsz = e // N_GROUPS
    sel = scores + bias
    neg = -jnp.inf
    sub = _row_index(gsz, t)
    gidx = _row_index(N_GROUPS, t)
    gscore = jnp.zeros((N_GROUPS, t), F32)
    for g in range(N_GROUPS):
        v = sel[g * gsz:(g + 1) * gsz, :]
        m1, i1 = _first_index_of_max(v, sub, gsz)
        m2 = jnp.max(jnp.where(sub == i1, neg, v), axis=0, keepdims=True)
        gscore = jnp.where(gidx == float(g), m1 + m2, gscore)
    gmask = jnp.zeros((N_GROUPS, t), F32)
    for _ in range(TOPK_GROUPS):
        _, i1 = _first_index_of_max(gscore, gidx, N_GROUPS)
        hit = gidx == i1
        gmask = jnp.where(hit, 1.0, gmask)
        gscore = jnp.where(hit, neg, gscore)
    emask = jnp.concatenate(
        [jnp.broadcast_to(gmask[g:g + 1, :], (gsz, t)) for g in range(N_GROUPS)], axis=0)
    cand = jnp.where(emask > 0.0, sel, neg)
    eidx = _row_index(e, t)
    kidx = _row_index(TOP_K, t)
    ids = jnp.zeros((TOP_K, t), F32)
    w = jnp.zeros((e, t), F32)
    member = jnp.zeros((e, t), F32)
    for k in range(TOP_K):
        _, i1 = _first_index_of_max(cand, eidx, e)
        hit = eidx == i1
        ids = jnp.where(kidx == float(k), i1, ids)
        w = jnp.where(hit, scores, w)
        member = jnp.where(hit, 1.0, member)
        cand = jnp.where(hit, neg, cand)
    return ids, w / jnp.sum(w, axis=0, keepdims=True) * ROUTED_SCALE, member


def _mix_kernel(osb_ref, og_ref, gsb_ref, ggla_ref, x_ref, mod_ref, wbs_ref, wbg_ref, wout_ref,
                g1_ref, b1_ref, wrh_ref, wrl_ref, rb_ref,
                x1_ref, h2_ref, h2row_ref, grow_ref, ids_ref, rank_ref, cnt_ref, count_ref,
                *, tiles_per_chunk):
    @pl.when(pl.program_id(0) % tiles_per_chunk == 0)
    def _():
        count_ref[...] = jnp.zeros_like(count_ref)

    mod = mod_ref[0]
    gate1, shift2, scale2 = mod[2:3, :], mod[3:4, :], mod[4:5, :]
    y = (gsb_ref[...].astype(F32) * _dot(osb_ref[...], wbs_ref[...])
         + ggla_ref[...].astype(F32) * _dot(og_ref[...], wbg_ref[...]))
    mix = _dot(y.astype(BF16), wout_ref[...])
    x1 = _layer_norm(DN_ALPHA * x_ref[...] + gate1 * mix) * g1_ref[...] + b1_ref[...]
    x1_ref[...] = x1
    h2 = _layer_norm(x1) * (1.0 + scale2) + shift2
    h_hi, h_lo = _split_hi_lo(h2)
    h2_ref[...] = h_hi
    _store_row_groups(h2row_ref, h_hi.astype(F32))
    logits_t = _dot_nt(wrh_ref[...], h_hi) + _dot_nt(wrh_ref[...], h_lo) + _dot_nt(wrl_ref[...], h_hi)
    ids, gates, member = _route(jax.nn.sigmoid(logits_t), rb_ref[...])
    n_e, tm = gates.shape
    padded = jnp.concatenate([gates, jnp.zeros((LANES - n_e, tm), F32)], axis=0)
    grow_ref[...] = padded.T.reshape(grow_ref.shape)
    r = lax.broadcasted_iota(jnp.int32, (tm, tm), 0)
    c = lax.broadcasted_iota(jnp.int32, (tm, tm), 1)
    seen = _dot(member.astype(BF16), (r <= c).astype(BF16)) + count_ref[...]
    eidx = _row_index(n_e, tm)
    kidx = _row_index(TOP_K, tm)
    rank = jnp.zeros((TOP_K, tm), F32)
    for k in range(TOP_K):
        mine = jnp.sum(jnp.where(eidx == ids[k:k + 1, :], seen, 0.0), axis=0, keepdims=True)
        rank = jnp.where(kidx == float(k), mine - 1.0, rank)
    count_ref[...] += jnp.sum(member, axis=1, keepdims=True)
    cnt_ref[...] = jnp.broadcast_to(count_ref[...], cnt_ref.shape)
    ids_ref[...] = ids.astype(jnp.int32)
    rank_ref[...] = rank.astype(jnp.int32)


def _mix(osb, og, gsb, ggla, x2, mod3, w_bsb, w_bgla, w_out, ln_g, ln_b, wr_hi, wr_lo, rbias, seq, tm, ct):
    n, d = x2.shape
    n_e = rbias.shape[0]
    const = lambda a: pl.BlockSpec(a.shape, lambda i: (0,) * a.ndim, pipeline_mode=pl.Buffered(1))
    row = lambda w: pl.BlockSpec((tm, w), lambda i: (i, 0))
    row3 = lambda w: pl.BlockSpec((tm, 1, w), lambda i: (i, 0, 0))
    col = pl.BlockSpec((TOP_K, tm), lambda i: (0, i))
    return pl.pallas_call(
        functools.partial(_mix_kernel, tiles_per_chunk=ct // tm),
        grid=(n // tm,),
        in_specs=[row(osb.shape[1]), row(og.shape[1]), row(d), row(d), row(d),
                  pl.BlockSpec((1, 6, d), lambda i: (i * tm // seq, 0, 0)),
                  const(w_bsb), const(w_bgla), const(w_out), const(ln_g), const(ln_b),
                  const(wr_hi), const(wr_lo), const(rbias)],
        out_specs=[row(d), row(d), pl.BlockSpec((tm * SUBLANES, LANES), lambda i: (i, 0)), row3(LANES), col, col,
                   pl.BlockSpec((n_e, LANES), lambda i: (0, i))],
        out_shape=[jax.ShapeDtypeStruct((n, d), F32), jax.ShapeDtypeStruct((n, d), BF16),
                   jax.ShapeDtypeStruct((n * SUBLANES, LANES), F32), jax.ShapeDtypeStruct((n, 1, LANES), F32),
                   jax.ShapeDtypeStruct((TOP_K, n), jnp.int32), jax.ShapeDtypeStruct((TOP_K, n), jnp.int32),
                   jax.ShapeDtypeStruct((n_e, n // tm * LANES), F32)],
        scratch_shapes=[pltpu.VMEM((n_e, 1), F32)],
        compiler_params=_params(("arbitrary",)),
        name="mix",
    )(osb, og, gsb, ggla, x2, mod3, w_bsb, w_bgla, w_out, ln_g, ln_b, wr_hi, wr_lo, rbias)


MOE_TM = MXU_DIM
MOE_CHUNK = 4096
MOE_DISPATCH_SPLIT = 4


def _run_offsets(cnt_tiles, ids_t, rank_t, ct, tiles_per_chunk):
    n_e = cnt_tiles.shape[0]
    n = ids_t.shape[1]
    cnt = cnt_tiles.reshape(n_e, -1, LANES)[:, tiles_per_chunk - 1::tiles_per_chunk, 0].T.astype(jnp.int32)
    off = jnp.cumsum(cnt, axis=1) - cnt
    chunk = (jnp.arange(n, dtype=jnp.int32) // ct)[None, :]
    dest = jnp.take(off.reshape(-1), chunk * n_e + ids_t) + rank_t
    return off.reshape(-1), cnt.reshape(-1), dest.T.reshape(-1)


def _dispatch_kernel(dest_ref, tab_ref):
    s = pl.program_id(1)
    k = TOP_K
    w = dest_ref.shape[0] // k
    pad = tab_ref.shape[2] - k * w * MOE_DISPATCH_SPLIT

    @pl.when(s == 0)
    def _():
        for i in range(pad):
            tab_ref[0, 0, tab_ref.shape[2] - 1 - i] = 0

    def body(t, _):
        for kk in range(k):
            tab_ref[0, 0, dest_ref[t * k + kk]] = s * w + t
        return 0

    lax.fori_loop(0, w, body, 0, unroll=4)


def _dispatch(dest, ct, tm):
    k = TOP_K
    n = dest.shape[0] // k
    w = ct // MOE_DISPATCH_SPLIT
    rows = ct * k + tm
    return pl.pallas_call(
        _dispatch_kernel,
        grid=(n // ct, MOE_DISPATCH_SPLIT),
        in_specs=[pl.BlockSpec((w * k,), lambda c, s: (c * MOE_DISPATCH_SPLIT + s,), memory_space=pltpu.SMEM)],
        out_specs=pl.BlockSpec((1, 1, rows), lambda c, s: (c, 0, 0), memory_space=pltpu.SMEM),
        out_shape=jax.ShapeDtypeStruct((n // ct, 1, rows), jnp.int32),
        compiler_params=_params(("arbitrary", "arbitrary")),
        name="dispatch",
    )(dest)


def _moe_kernel(off_ref, cnt_ref, tab_ref, h_ref, g_ref, wgu_ref, wd_ref, out_ref,
                xrow_ref, grow_ref, g2_ref, yrow_ref, wgub_ref, wdb_ref, prev_ref):
    c, e, n_e = pl.program_id(0), pl.program_id(1), pl.num_programs(1)
    tm = grow_ref.shape[0]
    f = wd_ref.shape[1]

    def group(i):
        return pl.ds(pl.multiple_of(i * SUBLANES, SUBLANES), SUBLANES)

    def gather(base):
        for r in range(tm):
            token = tab_ref[0, 0, base + r]
            xrow_ref[group(r), :] = h_ref[group(token), :]
            grow_ref[r] = g_ref[token]

    def scatter(base):
        for r0 in range(0, tm, 8):
            tokens = [tab_ref[0, 0, base + r0 + u] for u in range(8)]
            olds = [out_ref[group(t), :] for t in tokens]
            for u in reversed(range(8)):
                out_ref[group(tokens[u]), :] = olds[u] + yrow_ref[group(r0 + u), :]

    @pl.when(e == 0)
    def _():
        out_ref[...] = jnp.zeros_like(out_ref)
        yrow_ref[...] = jnp.zeros_like(yrow_ref)
        prev_ref[0] = 0
        gather(0)

    wgub_ref[...] = wgu_ref[0].astype(BF16)
    wdb_ref[...] = wd_ref[0].astype(BF16)
    n_rows = cnt_ref[c * n_e + e]
    first = off_ref[c * n_e + e]
    end = first + n_rows
    lane = lax.broadcasted_iota(jnp.int32, (1, LANES), 1)
    row_id = lax.broadcasted_iota(jnp.int32, (tm, 1), 0)

    def tile(i, _):
        base = first + i * tm
        x = _load_row_groups(xrow_ref).astype(BF16)
        g2_ref[...] = grow_ref[...].reshape(tm, LANES)
        scatter(prev_ref[0])
        gather(jnp.minimum(base + tm, end))
        gate = jnp.sum(jnp.where(lane == e, g2_ref[...], 0.0), axis=1, keepdims=True)
        gate = jnp.where(row_id < end - base, gate, 0.0)
        gu = _dot(x, wgub_ref[...])
        act = _silu(gu[:, :f]) * gu[:, f:] * gate
        _store_row_groups(yrow_ref, _dot(act.astype(BF16), wdb_ref[...]))
        prev_ref[0] = base
        return 0

    lax.fori_loop(0, (n_rows + tm - 1) // tm, tile, 0)

    @pl.when(e == n_e - 1)
    def _():
        scatter(prev_ref[0])


def _moe(h2row, grow, table, off, cnt, w_gu, w_down, ct, tm):
    n = grow.shape[0]
    n_e, d, f2 = w_gu.shape
    f = f2 // 2
    rows = table.shape[2]
    once = dict(pipeline_mode=pl.Buffered(1))
    grid_spec = pltpu.PrefetchScalarGridSpec(
        num_scalar_prefetch=2,
        grid=(n // ct, n_e),
        in_specs=[pl.BlockSpec((1, 1, rows), lambda c, e, *_: (c, 0, 0), memory_space=pltpu.SMEM),
                  pl.BlockSpec((ct * SUBLANES, LANES), lambda c, e, *_: (c, 0), **once),
                  pl.BlockSpec((ct, 1, LANES), lambda c, e, *_: (c, 0, 0), **once),
                  pl.BlockSpec((1, d, f2), lambda c, e, *_: (e, 0, 0)),
                  pl.BlockSpec((1, f, d), lambda c, e, *_: (e, 0, 0))],
        out_specs=pl.BlockSpec((ct * SUBLANES, LANES), lambda c, e, *_: (c, 0), **once),
        scratch_shapes=[pltpu.VMEM((tm * SUBLANES, LANES), F32),
                        pltpu.VMEM((tm, 1, LANES), F32), pltpu.VMEM((tm, LANES), F32),
                        pltpu.VMEM((tm * SUBLANES, LANES), F32),
                        pltpu.VMEM((d, f2), BF16), pltpu.VMEM((f, d), BF16), pltpu.SMEM((1,), jnp.int32)])
    return pl.pallas_call(
        _moe_kernel,
        grid_spec=grid_spec,
        out_shape=jax.ShapeDtypeStruct((n * SUBLANES, LANES), F32),
        compiler_params=_params(("arbitrary", "arbitrary")),
        name="moe",
    )(off, cnt, table, h2row, grow, w_gu, w_down)


def _post_kernel(routed_ref, h2_ref, x1_ref, mod_ref, wsgu_ref, wsd_ref, g2_ref, b2_ref, o_ref):
    f = wsd_ref.shape[0]
    gu = _dot(h2_ref[...], wsgu_ref[...])
    shared = _dot((_silu(gu[:, :f]) * gu[:, f:]).astype(BF16), wsd_ref[...])
    gate2 = mod_ref[0][5:6, :]
    ffn = _load_row_groups(routed_ref) + shared
    o_ref[...] = _layer_norm(DN_ALPHA * x1_ref[...] + gate2 * ffn) * g2_ref[...] + b2_ref[...]


def _post(routed, h2, x1, mod3, ws_gu, ws_down, ln_g, ln_b, seq, tm):
    n, d = h2.shape
    const = lambda a: pl.BlockSpec(a.shape, lambda i: (0,) * a.ndim, pipeline_mode=pl.Buffered(1))
    row = lambda w: pl.BlockSpec((tm, w), lambda i: (i, 0))
    return pl.pallas_call(
        _post_kernel,
        grid=(n // tm,),
        in_specs=[pl.BlockSpec((tm * SUBLANES, LANES), lambda i: (i, 0)), row(d), row(d),
                  pl.BlockSpec((1, 6, d), lambda i: (i * tm // seq, 0, 0)),
                  const(ws_gu), const(ws_down), const(ln_g), const(ln_b)],
        out_specs=row(d),
        out_shape=jax.ShapeDtypeStruct((n, d), F32),
        compiler_params=_params(("parallel",)),
        name="post",
    )(routed, h2, x1, mod3, ws_gu, ws_down, ln_g, ln_b)


def _layer(x, c, w_ada, b_ada, w_in, gla_w_gate_up, gla_b_gate, gla_norm_w, w_branch_sb, w_branch_gla,
           w_out, ln1_g, ln1_b, w_router, router_bias, w_exp_gate_up, w_exp_down, w_shared_gate_up,
           w_shared_down, ln2_g, ln2_b):
    bsz, seq, d = x.shape
    n = bsz * seq
    x2 = x.reshape(n, d)
    mod3 = _ada(c, w_ada, b_ada).reshape(bsz, 6, d)

    sbw = SB_HEADS * SB_HEAD_DIM
    gkw = GLA_HEADS * GLA_KEY_DIM
    gvw = GLA_HEADS * GLA_VALUE_DIM
    n_a = 3 * sbw + 2 * gkw + 2 * gvw
    w_a = w_in[:, :n_a].astype(BF16)
    w_glr = jnp.pad(w_in[:, n_a:n_a + GLA_GATE_RANK], ((0, 0), (0, LANES - GLA_GATE_RANK))).astype(BF16)
    w_gu = jnp.pad(gla_w_gate_up, ((0, LANES - GLA_GATE_RANK), (0, 0))).astype(BF16)
    w_merge = w_in[:, n_a + GLA_GATE_RANK:].astype(BF16)
    tm = min(512, seq)
    (q_sb, k_sb, v_sb, q_g, k_g, v_g, r_g, log_a, gate_sb, gate_gla) = _inproj(
        x2, mod3, w_a, w_glr, w_gu, gla_b_gate.reshape(1, gkw), w_merge, seq, tm)

    o_sb = _sb_attention(q_sb, k_sb, v_sb, bsz, seq, min(MXU_DIM, seq))
    o_g = _gla(q_g, k_g, v_g, log_a, r_g, gla_norm_w.reshape(1, gvw), bsz, seq)

    wr_t = w_router.T
    wr_hi = wr_t.astype(BF16)
    wr_lo = (wr_t - wr_hi.astype(F32)).astype(BF16)
    ct = min(MOE_CHUNK, n)
    x1, h2, h2row, grow, ids_t, rank_t, cnt_tiles = _mix(
        o_sb, o_g, gate_sb, gate_gla, x2, mod3,
        w_branch_sb.astype(BF16), w_branch_gla.astype(BF16), w_out.astype(BF16),
        ln1_g.reshape(1, d), ln1_b.reshape(1, d), wr_hi, wr_lo,
        router_bias.reshape(N_EXPERTS, 1), seq, tm, ct)

    off, cnt, dest = _run_offsets(cnt_tiles, ids_t, rank_t, ct, ct // tm)
    table = _dispatch(dest, ct, MOE_TM)
    routed = _moe(h2row, grow, table, off, cnt, w_exp_gate_up, w_exp_down, ct, MOE_TM)
    out = _post(routed, h2, x1, mod3, w_shared_gate_up.astype(BF16), w_shared_down.astype(BF16),
                ln2_g.reshape(1, d), ln2_b.reshape(1, d), seq, tm)
    return out.reshape(bsz, seq, d)


def kernel(x, c, w_ada, b_ada, w_in, gla_w_gate_up, gla_b_gate, gla_norm_w, w_branch_sb, w_branch_gla,
           w_out, ln1_g, ln1_b, w_router, router_bias, w_exp_gate_up, w_exp_down, w_shared_gate_up,
           w_shared_down, ln2_g, ln2_b):
    depth = w_ada.shape[0]
    for l in range(depth):
        x = _layer(x, c, w_ada[l], b_ada[l], w_in[l], gla_w_gate_up[l], gla_b_gate[l], gla_norm_w[l],
                   w_branch_sb[l], w_branch_gla[l], w_out[l], ln1_g[l], ln1_b[l], w_router[l],
                   router_bias[l], w_exp_gate_up[l], w_exp_down[l], w_shared_gate_up[l],
                   w_shared_down[l], ln2_g[l], ln2_b[l])
    return x
```

```python
import functools
import math

import jax
import jax.numpy as jnp
from jax import lax
from jax.experimental import pallas as pl
from jax.experimental.pallas import tpu as pltpu

F32 = jnp.float32
BF16 = jnp.bfloat16

SB_HEADS = 8
SB_HEAD_DIM = 64
GLA_HEADS = 4
GLA_KEY_DIM = 64
GLA_VALUE_DIM = 128
GLA_CHUNK = 64
GLA_GATE_RANK = 16
GLA_GATE_TEMP = 16.0
N_EXPERTS = 64
TOP_K = 8
N_GROUPS = 8
TOPK_GROUPS = 4
ROUTED_SCALE = 2.5
DEPTH = 1
DN_ALPHA = (2 * DEPTH) ** 0.25
LN_EPS = 1e-5

LANES = 128
MXU_DIM = 256
VMEM_LIMIT = 56 * 1024 * 1024


def _dot(a, b):
    return jnp.dot(a, b, preferred_element_type=F32)


def _dot_nt(a, b):
    return lax.dot_general(a, b, (((1,), (1,)), ((), ())), preferred_element_type=F32)


def _split_hi_lo(x):
    hi = x.astype(BF16)
    lo = (x - hi.astype(F32)).astype(BF16)
    return hi, lo


def _silu(x):
    return x * jax.nn.sigmoid(x)


def _layer_norm(x):
    mu = jnp.mean(x, axis=-1, keepdims=True)
    xc = x - mu
    var = jnp.mean(xc * xc, axis=-1, keepdims=True)
    return xc * lax.rsqrt(var + LN_EPS)


SUBLANES = 8


def _store_row_groups(ref, x):
    m = x.shape[0]
    for j in range(SUBLANES):
        ref[pl.ds(j, m, stride=SUBLANES), :] = x[:, j * LANES:(j + 1) * LANES]


def _load_row_groups(ref):
    m = ref.shape[0] // SUBLANES
    return jnp.concatenate([ref[pl.ds(j, m, stride=SUBLANES), :] for j in range(SUBLANES)], axis=1)


def _params(sem):
    return pltpu.CompilerParams(dimension_semantics=sem, vmem_limit_bytes=VMEM_LIMIT)


def _ada_kernel(c_ref, w_ref, b_ref, o_ref):
    s_hi, s_lo = _split_hi_lo(_silu(c_ref[...]))
    w_hi, w_lo = _split_hi_lo(w_ref[...])
    acc = _dot(s_hi, w_hi) + _dot(s_lo, w_hi) + _dot(s_hi, w_lo)
    o_ref[...] = acc + b_ref[...]


def _ada(c, w_ada, b_ada):
    bsz, d = c.shape
    n_out = w_ada.shape[1]
    tn = d
    return pl.pallas_call(
        _ada_kernel,
        grid=(n_out // tn,),
        in_specs=[pl.BlockSpec((bsz, d), lambda j: (0, 0)),
                  pl.BlockSpec((d, tn), lambda j: (0, j)),
                  pl.BlockSpec((1, tn), lambda j: (0, j))],
        out_specs=pl.BlockSpec((bsz, tn), lambda j: (0, j)),
        out_shape=jax.ShapeDtypeStruct((bsz, n_out), F32),
        compiler_params=_params(("parallel",)),
        name="ada",
    )(c, w_ada, b_ada.reshape(1, n_out))


def _inproj_kernel(x_ref, mod_ref, wa_ref, wglr_ref, wgu_ref, bg_ref, wm_ref,
                   qsb_ref, ksb_ref, vsb_ref, qg_ref, kg_ref, vg_ref, rg_ref, la_ref,
                   gsb_ref, ggla_ref):
    mod = mod_ref[0]
    shift1, scale1 = mod[0:1, :], mod[1:2, :]
    h = _layer_norm(x_ref[...]) * (1.0 + scale1) + shift1
    hb = h.astype(BF16)
    sbw = SB_HEADS * SB_HEAD_DIM
    gkw = GLA_HEADS * GLA_KEY_DIM
    gvw = GLA_HEADS * GLA_VALUE_DIM
    off = 0
    qsb_ref[...] = (_dot(hb, wa_ref[:, off:off + sbw]) * (SB_HEAD_DIM ** -0.5)).astype(BF16); off += sbw
    ksb_ref[...] = _dot(hb, wa_ref[:, off:off + sbw]).astype(BF16); off += sbw
    vsb_ref[...] = _dot(hb, wa_ref[:, off:off + sbw]).astype(BF16); off += sbw
    qg_ref[...] = (_dot(hb, wa_ref[:, off:off + gkw]) * (GLA_KEY_DIM ** -0.5)).astype(BF16); off += gkw
    kg_ref[...] = _dot(hb, wa_ref[:, off:off + gkw]).astype(BF16); off += gkw
    vg_ref[...] = _dot(hb, wa_ref[:, off:off + gvw]).astype(BF16); off += gvw
    rg_ref[...] = _silu(_dot(hb, wa_ref[:, off:off + gvw])).astype(BF16)
    g_lr = _dot(hb, wglr_ref[...])
    z = _dot(g_lr.astype(BF16), wgu_ref[...]) + bg_ref[...]
    la_ref[...] = (jnp.minimum(z, 0.0) - jnp.log(1.0 + jnp.exp(-jnp.abs(z)))) * (1.0 / GLA_GATE_TEMP)
    d = gsb_ref.shape[1]
    gsb_ref[...] = jax.nn.sigmoid(_dot(hb, wm_ref[:, 0:d])).astype(BF16)
    ggla_ref[...] = jax.nn.sigmoid(_dot(hb, wm_ref[:, d:2 * d])).astype(BF16)


def _inproj(x2, mod3, w_a, w_glr, w_gu, b_gate, w_merge, seq, tm):
    n, d = x2.shape
    sbw = SB_HEADS * SB_HEAD_DIM
    gkw = GLA_HEADS * GLA_KEY_DIM
    gvw = GLA_HEADS * GLA_VALUE_DIM
    const = lambda shape: pl.BlockSpec(shape, lambda i: (0,) * len(shape), pipeline_mode=pl.Buffered(1))
    row = lambda w: pl.BlockSpec((tm, w), lambda i: (i, 0))
    widths = (sbw, sbw, sbw, gkw, gkw, gvw, gvw)
    out_shape = [jax.ShapeDtypeStruct((n, w), BF16) for w in widths]
    out_shape += [jax.ShapeDtypeStruct((n, gkw), F32),
                  jax.ShapeDtypeStruct((n, d), BF16), jax.ShapeDtypeStruct((n, d), BF16)]
    out_specs = [row(w) for w in widths] + [row(gkw), row(d), row(d)]
    return pl.pallas_call(
        _inproj_kernel,
        grid=(n // tm,),
        in_specs=[row(d),
                  pl.BlockSpec((1, 6, d), lambda i: (i * tm // seq, 0, 0)),
                  const(w_a.shape), const(w_glr.shape), const(w_gu.shape), const(b_gate.shape),
                  const(w_merge.shape)],
        out_specs=out_specs,
        out_shape=out_shape,
        compiler_params=_params(("parallel",)),
        name="inproj",
    )(x2, mod3, w_a, w_glr, w_gu, b_gate, w_merge)


def _sb_kernel(q_ref, k_ref, v_ref, o_ref, z_ref, t_ref, rest_ref, acc_ref, carry_ref, *, tq):
    qi = pl.program_id(2)
    q = q_ref[...]
    lane = lax.broadcasted_iota(jnp.int32, (1, LANES), 1)
    head0 = lane < SB_HEAD_DIM
    zero = jnp.zeros_like(q)
    q2 = jnp.concatenate([jnp.where(head0, q, zero), jnp.where(head0, zero, q)], axis=0)
    r = lax.broadcasted_iota(jnp.int32, (2 * tq, tq), 0)
    c = lax.broadcasted_iota(jnp.int32, (2 * tq, tq), 1)
    upper = ((r % tq) > c).astype(BF16)

    def rows_of(j):
        return pl.ds(pl.multiple_of(jnp.maximum(j, 0) * tq, tq), tq)

    def scores(j):
        z_ref[...] = _dot_nt(q2, k_ref[rows_of(j), :])

    def log_terms(diag):
        z = z_ref[...]
        ls_neg = jnp.minimum(-z, 0.0) - jnp.log(1.0 + jnp.exp(-jnp.abs(z)))
        t = ls_neg + z + carry_ref[...]
        if diag:
            strict = c < (r % tq)
            ls_neg = jnp.where(strict, ls_neg, 0.0)
            t = jnp.where(strict, t, -jnp.inf)
        hi, lo = _split_hi_lo(ls_neg)
        rest_ref[...] = _dot(jnp.concatenate([hi, lo], axis=1), upper)
        t_ref[...] = t
        carry_ref[...] += jnp.sum(ls_neg, axis=1, keepdims=True)

    def weigh(j):
        w = jnp.exp(t_ref[...] + rest_ref[...])
        acc_ref[...] += _dot(w.astype(BF16), v_ref[rows_of(j), :])

    acc_ref[...] = jnp.zeros_like(acc_ref)
    carry_ref[...] = jnp.zeros_like(carry_ref)
    scores(qi)
    log_terms(True)
    scores(qi - 1)

    def body(i, _):
        weigh(qi - i)
        log_terms(False)
        scores(qi - 2 - i)
        return 0

    lax.fori_loop(0, qi, body, 0)
    weigh(0)
    acc = acc_ref[...]
    o_ref[...] = jnp.where(head0, acc[:tq], acc[tq:]).astype(BF16)


def _sb_attention(q, k, v, bsz, seq, tq):
    n, w = q.shape
    n_pairs = w // LANES
    nq = seq // tq
    return pl.pallas_call(
        functools.partial(_sb_kernel, tq=tq),
        grid=(bsz, n_pairs, nq),
        in_specs=[pl.BlockSpec((tq, LANES), lambda b, p, i: (b * nq + i, p)),
                  pl.BlockSpec((seq, LANES), lambda b, p, i: (b, p)),
                  pl.BlockSpec((seq, LANES), lambda b, p, i: (b, p))],
        out_specs=pl.BlockSpec((tq, LANES), lambda b, p, i: (b * nq + i, p)),
        out_shape=jax.ShapeDtypeStruct((n, w), BF16),
        scratch_shapes=[pltpu.VMEM((2 * tq, tq), F32), pltpu.VMEM((2 * tq, tq), F32),
                        pltpu.VMEM((2 * tq, tq), F32), pltpu.VMEM((2 * tq, LANES), F32),
                        pltpu.VMEM((2 * tq, 1), F32)],
        compiler_params=_params(("parallel", "parallel", "arbitrary")),
        name="sb_attn",
    )(q, k, v)


GLA_SLAB = MXU_DIM
GLA_SUB = 16


def _gla_constants():
    s, c, sb = GLA_SLAB, GLA_CHUNK, GLA_SUB
    t = lax.broadcasted_iota(jnp.int32, (s, s), 0)
    u = lax.broadcasted_iota(jnp.int32, (s, s), 1)
    same_chunk = (t // c) == (u // c)

    def cum_to(row):
        return (same_chunk & (u <= row)).astype(F32)

    g = cum_to(t)
    g_total = same_chunk.astype(F32)
    sub_start = (t // sb) * sb
    next_sub_start = jnp.minimum(sub_start + sb, (t // c) * c + c - 1)
    chunk_mid = (t // c) * c + c // 2
    mats = (g,
            g_total - g,
            g - cum_to(sub_start),
            cum_to(next_sub_start) - g,
            g - cum_to(chunk_mid))
    return [m.astype(BF16) for m in mats]


def _gla_kernel(q_ref, k_ref, v_ref, la_ref, rg_ref, nw_ref, o_ref, s_ref):
    s, c, sb = GLA_SLAB, GLA_CHUNK, GLA_SUB
    dk, dv, nh = GLA_KEY_DIM, GLA_VALUE_DIM, GLA_HEADS
    kw = nh * dk
    n_slabs = q_ref.shape[0] // s
    per = s // c
    m_g, m_kdec, m_q0, m_k1, m_q2 = _gla_constants()

    t = lax.broadcasted_iota(jnp.int32, (s, s), 0)
    u = lax.broadcasted_iota(jnp.int32, (s, s), 1)
    causal = ((t // c) == (u // c)) & (u <= t)
    bt, bu = (t % c) // sb, (u % c) // sb
    lvl0 = causal & (bt == bu)
    lvl1 = causal & ((bt % 2) == 1) & (bu == bt - 1)
    lvl2 = causal & (bt >= 2) & (bu <= 1)
    chunk_of_row_is_lane_block = (t // c) == (u // dk)
    lane_block_is_chunk_of_col = (t // dk) == (u // c)
    lane_kw = lax.broadcasted_iota(jnp.int32, (1, kw), 1)
    rowi = lax.broadcasted_iota(jnp.int32, (kw, kw), 0)
    coli = lax.broadcasted_iota(jnp.int32, (kw, kw), 1)

    s_ref[...] = jnp.zeros_like(s_ref)

    def slab(i, _):
        r0 = pl.multiple_of(i * s, s)
        rows = pl.ds(r0, s)
        la_hi, la_lo = _split_hi_lo(la_ref[rows, :])

        def cum(m):
            return _dot(m, la_hi) + _dot(m, la_lo)

        q = q_ref[rows, :].astype(F32)
        k = k_ref[rows, :].astype(F32)
        e_q0 = cum(m_q0)
        e_q2 = cum(m_q2)
        qg = (q * jnp.exp(cum(m_g))).astype(BF16)
        kdec = (k * jnp.exp(cum(m_kdec))).astype(BF16)
        q0 = q * jnp.exp(e_q0)
        k0 = (k * jnp.exp(-e_q0)).astype(BF16)
        k1 = (k * jnp.exp(cum(m_k1))).astype(BF16)
        q2 = q * jnp.exp(e_q2)
        k2 = (k * jnp.exp(-e_q2)).astype(BF16)
        g_last = cum(m_g + m_kdec)
        gl_rows = jnp.concatenate(
            [jnp.broadcast_to(g_last[j * c:j * c + 1, :], (LANES, kw)) for j in range(per)], axis=0)
        gl_hi, gl_lo = _split_hi_lo(gl_rows)

        for h in range(nh):
            in_head = (lane_kw >= h * dk) & (lane_kw < (h + 1) * dk)
            sel = ((rowi // dk == h) & (rowi % dk == coli % dk)).astype(BF16)
            sel_t = ((coli // dk == h) & (coli % dk == rowi % dk)).astype(BF16)
            v = v_ref[rows, h * dv:(h + 1) * dv]

            a0 = _dot_nt(jnp.where(in_head, q0, 0.0).astype(BF16), k0)
            a1 = _dot_nt(jnp.where(in_head, q0, 0.0).astype(BF16), k1)
            a2 = _dot_nt(jnp.where(in_head, q2, 0.0).astype(BF16), k2)
            a = jnp.where(lvl0, a0, jnp.where(lvl1, a1, jnp.where(lvl2, a2, 0.0)))
            o = _dot(a.astype(BF16), v)

            q_blk = jnp.where(chunk_of_row_is_lane_block, _dot(qg, sel), 0.0).astype(BF16)
            k_blk_t = jnp.where(lane_block_is_chunk_of_col, _dot_nt(sel_t, kdec), 0.0).astype(BF16)
            upd = _dot(k_blk_t, v)
            dec = jnp.exp(_dot_nt(sel_t[0:dk, :], gl_hi) + _dot_nt(sel_t[0:dk, :], gl_lo))
            state = s_ref[h]
            entering = []
            for j in range(per):
                entering.append(state)
                state = dec[:, j * LANES:(j + 1) * LANES] * state + upd[j * dk:(j + 1) * dk, :]
            s_ref[h] = state
            o = o + _dot(q_blk, jnp.concatenate(entering, axis=0).astype(BF16))

            o = o * lax.rsqrt(jnp.mean(o * o, axis=-1, keepdims=True) + LN_EPS)
            o = o * nw_ref[:, h * dv:(h + 1) * dv] * rg_ref[rows, h * dv:(h + 1) * dv].astype(F32)
            o_ref[rows, h * dv:(h + 1) * dv] = o.astype(BF16)
        return 0

    lax.fori_loop(0, n_slabs, slab, 0)


def _gla(q, k, v, la, rg, norm_w, bsz, seq):
    n, kw = q.shape
    vw = v.shape[1]
    blk = lambda w: pl.BlockSpec((seq, w), lambda b: (b, 0))
    return pl.pallas_call(
        _gla_kernel,
        grid=(bsz,),
        in_specs=[blk(kw), blk(kw), blk(vw), blk(kw), blk(vw),
                  pl.BlockSpec((1, vw), lambda b: (0, 0))],
        out_specs=blk(vw),
        out_shape=jax.ShapeDtypeStruct((n, vw), BF16),
        scratch_shapes=[pltpu.VMEM((GLA_HEADS, GLA_KEY_DIM, GLA_VALUE_DIM), F32)],
        compiler_params=_params(("parallel",)),
        name="gla",
    )(q, k, v, la, rg, norm_w)


def _first_index_of_max(v, idx, n):
    m = jnp.max(v, axis=0, keepdims=True)
    first = jnp.min(jnp.where(v == m, idx, float(n)), axis=0, keepdims=True)
    return m, first


def _row_index(n, t):
    return lax.broadcasted_iota(jnp.int32, (n, t), 0).astype(F32)


def _route(scores, bias):
    e, t = scores.shape
    gsz = e // N_GROUPS
    sel = scores + bias
    neg = -jnp.inf
    sub = _row_index(gsz, t)
    gidx = _row_index(N_GROUPS, t)
    gscore = jnp.zeros((N_GROUPS, t), F32)
    for g in range(N_GROUPS):
        v = sel[g * gsz:(g + 1) * gsz, :]
        m1, i1 = _first_index_of_max(v, sub, gsz)
        m2 = jnp.max(jnp.where(sub == i1, neg, v), axis=0, keepdims=True)
        gscore = jnp.where(gidx == float(g), m1 + m2, gscore)
    gmask = jnp.zeros((N_GROUPS, t), F32)
    for _ in range(TOPK_GROUPS):
        _, i1 = _first_index_of_max(gscore, gidx, N_GROUPS)
        hit = gidx == i1
        gmask = jnp.where(hit, 1.0, gmask)
        gscore = jnp.where(hit, neg, gscore)
    emask = jnp.concatenate(
        [jnp.broadcast_to(gmask[g:g + 1, :], (gsz, t)) for g in range(N_GROUPS)], axis=0)
    cand = jnp.where(emask > 0.0, sel, neg)
    eidx = _row_index(e, t)
    kidx = _row_index(TOP_K, t)
    ids = jnp.zeros((TOP_K, t), F32)
    w = jnp.zeros((e, t), F32)
    member = jnp.zeros((e, t), F32)
    for k in range(TOP_K):
        _, i1 = _first_index_of_max(cand, eidx, e)
        hit = eidx == i1
        ids = jnp.where(kidx == float(k), i1, ids)
        w = jnp.where(hit, scores, w)
        member = jnp.where(hit, 1.0, member)
        cand = jnp.where(hit, neg, cand)
    return ids, w / jnp.sum(w, axis=0, keepdims=True) * ROUTED_SCALE, member


def _mix_kernel(osb_ref, og_ref, gsb_ref, ggla_ref, x_ref, mod_ref, wbs_ref, wbg_ref, wout_ref,
                g1_ref, b1_ref, wrh_ref, wrl_ref, rb_ref,
                x1_ref, h2_ref, h2row_ref, grow_ref, ids_ref, rank_ref, cnt_ref, count_ref,
                *, tiles_per_chunk):
    @pl.when(pl.program_id(0) % tiles_per_chunk == 0)
    def _():
        count_ref[...] = jnp.zeros_like(count_ref)

    mod = mod_ref[0]
    gate1, shift2, scale2 = mod[2:3, :], mod[3:4, :], mod[4:5, :]
    y = (gsb_ref[...].astype(F32) * _dot(osb_ref[...], wbs_ref[...])
         + ggla_ref[...].astype(F32) * _dot(og_ref[...], wbg_ref[...]))
    mix = _dot(y.astype(BF16), wout_ref[...])
    x1 = _layer_norm(DN_ALPHA * x_ref[...] + gate1 * mix) * g1_ref[...] + b1_ref[...]
    x1_ref[...] = x1
    h2 = _layer_norm(x1) * (1.0 + scale2) + shift2
    h_hi, h_lo = _split_hi_lo(h2)
    h2_ref[...] = h_hi
    _store_row_groups(h2row_ref, h_hi.astype(F32))
    logits_t = _dot_nt(wrh_ref[...], h_hi) + _dot_nt(wrh_ref[...], h_lo) + _dot_nt(wrl_ref[...], h_hi)
    ids, gates, member = _route(jax.nn.sigmoid(logits_t), rb_ref[...])
    n_e, tm = gates.shape
    padded = jnp.concatenate([gates, jnp.zeros((LANES - n_e, tm), F32)], axis=0)
    grow_ref[...] = padded.T.reshape(grow_ref.shape)
    r = lax.broadcasted_iota(jnp.int32, (tm, tm), 0)
    c = lax.broadcasted_iota(jnp.int32, (tm, tm), 1)
    seen = _dot(member.astype(BF16), (r <= c).astype(BF16)) + count_ref[...]
    eidx = _row_index(n_e, tm)
    kidx = _row_index(TOP_K, tm)
    rank = jnp.zeros((TOP_K, tm), F32)
    for k in range(TOP_K):
        mine = jnp.sum(jnp.where(eidx == ids[k:k + 1, :], seen, 0.0), axis=0, keepdims=True)
        rank = jnp.where(kidx == float(k), mine - 1.0, rank)
    count_ref[...] += jnp.sum(member, axis=1, keepdims=True)
    cnt_ref[...] = jnp.broadcast_to(count_ref[...], cnt_ref.shape)
    ids_ref[...] = ids.astype(jnp.int32)
    rank_ref[...] = rank.astype(jnp.int32)


def _mix(osb, og, gsb, ggla, x2, mod3, w_bsb, w_bgla, w_out, ln_g, ln_b, wr_hi, wr_lo, rbias, seq, tm, ct):
    n, d = x2.shape
    n_e = rbias.shape[0]
    const = lambda a: pl.BlockSpec(a.shape, lambda i: (0,) * a.ndim, pipeline_mode=pl.Buffered(1))
    row = lambda w: pl.BlockSpec((tm, w), lambda i: (i, 0))
    row3 = lambda w: pl.BlockSpec((tm, 1, w), lambda i: (i, 0, 0))
    col = pl.BlockSpec((TOP_K, tm), lambda i: (0, i))
    return pl.pallas_call(
        functools.partial(_mix_kernel, tiles_per_chunk=ct // tm),
        grid=(n // tm,),
        in_specs=[row(osb.shape[1]), row(og.shape[1]), row(d), row(d), row(d),
                  pl.BlockSpec((1, 6, d), lambda i: (i * tm // seq, 0, 0)),
                  const(w_bsb), const(w_bgla), const(w_out), const(ln_g), const(ln_b),
                  const(wr_hi), const(wr_lo), const(rbias)],
        out_specs=[row(d), row(d), pl.BlockSpec((tm * SUBLANES, LANES), lambda i: (i, 0)), row3(LANES), col, col,
                   pl.BlockSpec((n_e, LANES), lambda i: (0, i))],
        out_shape=[jax.ShapeDtypeStruct((n, d), F32), jax.ShapeDtypeStruct((n, d), BF16),
                   jax.ShapeDtypeStruct((n * SUBLANES, LANES), F32), jax.ShapeDtypeStruct((n, 1, LANES), F32),
                   jax.ShapeDtypeStruct((TOP_K, n), jnp.int32), jax.ShapeDtypeStruct((TOP_K, n), jnp.int32),
                   jax.ShapeDtypeStruct((n_e, n // tm * LANES), F32)],
        scratch_shapes=[pltpu.VMEM((n_e, 1), F32)],
        compiler_params=_params(("arbitrary",)),
        name="mix",
    )(osb, og, gsb, ggla, x2, mod3, w_bsb, w_bgla, w_out, ln_g, ln_b, wr_hi, wr_lo, rbias)


MOE_TM = MXU_DIM
MOE_CHUNK = 4096
MOE_DISPATCH_SPLIT = 4


def _run_offsets(cnt_tiles, ids_t, rank_t, ct, tiles_per_chunk):
    n_e = cnt_tiles.shape[0]
    n = ids_t.shape[1]
    cnt = cnt_tiles.reshape(n_e, -1, LANES)[:, tiles_per_chunk - 1::tiles_per_chunk, 0].T.astype(jnp.int32)
    off = jnp.cumsum(cnt, axis=1) - cnt
    run_start = jnp.repeat(off.T, ct, axis=1)
    experts = jnp.arange(n_e, dtype=jnp.int32)[:, None, None]
    dest = jnp.sum(jnp.where(ids_t[None] == experts, run_start[:, None, :], 0), axis=0) + rank_t
    return off.reshape(-1), cnt.reshape(-1), [dest[k] for k in range(dest.shape[0])]


def _dispatch_kernel(*refs):
    dest_refs, tab_ref = refs[:-1], refs[-1]
    s = pl.program_id(1)
    w = dest_refs[0].shape[0]
    pad = tab_ref.shape[2] - len(dest_refs) * w * MOE_DISPATCH_SPLIT

    @pl.when(s == 0)
    def _():
        for i in range(pad):
            tab_ref[0, 0, tab_ref.shape[2] - 1 - i] = 0

    def body(t, _):
        for dest_ref in dest_refs:
            tab_ref[0, 0, dest_ref[t]] = s * w + t
        return 0

    lax.fori_loop(0, w, body, 0, unroll=4)


def _dispatch(dest, ct, tm):
    k = len(dest)
    n = dest[0].shape[0]
    w = ct // MOE_DISPATCH_SPLIT
    rows = ct * k + tm
    index = pl.BlockSpec((w,), lambda c, s: (c * MOE_DISPATCH_SPLIT + s,), memory_space=pltpu.SMEM)
    return pl.pallas_call(
        _dispatch_kernel,
        grid=(n // ct, MOE_DISPATCH_SPLIT),
        in_specs=[index] * k,
        out_specs=pl.BlockSpec((1, 1, rows), lambda c, s: (c, 0, 0), memory_space=pltpu.SMEM),
        out_shape=jax.ShapeDtypeStruct((n // ct, 1, rows), jnp.int32),
        compiler_params=_params(("arbitrary", "arbitrary")),
        name="dispatch",
    )(*dest)


def _moe_kernel(off_ref, cnt_ref, tab_ref, h_ref, g_ref, wgu_ref, wd_ref, out_ref,
                xrow_ref, grow_ref, g2_ref, yrow_ref, wgub_ref, wdb_ref, prev_ref):
    c, e, n_e = pl.program_id(0), pl.program_id(1), pl.num_programs(1)
    tm = grow_ref.shape[0]
    f = wd_ref.shape[1]

    def group(i):
        return pl.ds(pl.multiple_of(i * SUBLANES, SUBLANES), SUBLANES)

    def gather(base):
        for r in range(tm):
            token = tab_ref[0, 0, base + r]
            xrow_ref[group(r), :] = h_ref[group(token), :]
            grow_ref[r] = g_ref[token]

    def scatter(base):
        for r0 in range(0, tm, 8):
            tokens = [tab_ref[0, 0, base + r0 + u] for u in range(8)]
            olds = [out_ref[group(t), :] for t in tokens]
            for u in reversed(range(8)):
                out_ref[group(tokens[u]), :] = olds[u] + yrow_ref[group(r0 + u), :]

    @pl.when(e == 0)
    def _():
        out_ref[...] = jnp.zeros_like(out_ref)
        yrow_ref[...] = jnp.zeros_like(yrow_ref)
        prev_ref[0] = 0
        gather(0)

    wgub_ref[...] = wgu_ref[0].astype(BF16)
    wdb_ref[...] = wd_ref[0].astype(BF16)
    n_rows = cnt_ref[c * n_e + e]
    first = off_ref[c * n_e + e]
    end = first + n_rows
    lane = lax.broadcasted_iota(jnp.int32, (1, LANES), 1)
    row_id = lax.broadcasted_iota(jnp.int32, (tm, 1), 0)

    def tile(i, _):
        base = first + i * tm
        x = _load_row_groups(xrow_ref).astype(BF16)
        g2_ref[...] = grow_ref[...].reshape(tm, LANES)
        scatter(prev_ref[0])
        gather(jnp.minimum(base + tm, end))
        gate = jnp.sum(jnp.where(lane == e, g2_ref[...], 0.0), axis=1, keepdims=True)
        gate = jnp.where(row_id < end - base, gate, 0.0)
        gu = _dot(x, wgub_ref[...])
        act = _silu(gu[:, :f]) * gu[:, f:] * gate
        _store_row_groups(yrow_ref, _dot(act.astype(BF16), wdb_ref[...]))
        prev_ref[0] = base
        return 0

    lax.fori_loop(0, (n_rows + tm - 1) // tm, tile, 0)

    @pl.when(e == n_e - 1)
    def _():
        scatter(prev_ref[0])


def _moe(h2row, grow, table, off, cnt, w_gu, w_down, ct, tm):
    n = grow.shape[0]
    n_e, d, f2 = w_gu.shape
    f = f2 // 2
    rows = table.shape[2]
    once = dict(pipeline_mode=pl.Buffered(1))
    grid_spec = pltpu.PrefetchScalarGridSpec(
        num_scalar_prefetch=2,
        grid=(n // ct, n_e),
        in_specs=[pl.BlockSpec((1, 1, rows), lambda c, e, *_: (c, 0, 0), memory_space=pltpu.SMEM),
                  pl.BlockSpec((ct * SUBLANES, LANES), lambda c, e, *_: (c, 0), **once),
                  pl.BlockSpec((ct, 1, LANES), lambda c, e, *_: (c, 0, 0), **once),
                  pl.BlockSpec((1, d, f2), lambda c, e, *_: (e, 0, 0)),
                  pl.BlockSpec((1, f, d), lambda c, e, *_: (e, 0, 0))],
        out_specs=pl.BlockSpec((ct * SUBLANES, LANES), lambda c, e, *_: (c, 0), **once),
        scratch_shapes=[pltpu.VMEM((tm * SUBLANES, LANES), F32),
                        pltpu.VMEM((tm, 1, LANES), F32), pltpu.VMEM((tm, LANES), F32),
                        pltpu.VMEM((tm * SUBLANES, LANES), F32),
                        pltpu.VMEM((d, f2), BF16), pltpu.VMEM((f, d), BF16), pltpu.SMEM((1,), jnp.int32)])
    return pl.pallas_call(
        _moe_kernel,
        grid_spec=grid_spec,
        out_shape=jax.ShapeDtypeStruct((n * SUBLANES, LANES), F32),
        compiler_params=_params(("arbitrary", "arbitrary")),
        name="moe",
    )(off, cnt, table, h2row, grow, w_gu, w_down)


def _post_kernel(routed_ref, h2_ref, x1_ref, mod_ref, wsgu_ref, wsd_ref, g2_ref, b2_ref, o_ref):
    f = wsd_ref.shape[0]
    gu = _dot(h2_ref[...], wsgu_ref[...])
    shared = _dot((_silu(gu[:, :f]) * gu[:, f:]).astype(BF16), wsd_ref[...])
    gate2 = mod_ref[0][5:6, :]
    ffn = _load_row_groups(routed_ref) + shared
    o_ref[...] = _layer_norm(DN_ALPHA * x1_ref[...] + gate2 * ffn) * g2_ref[...] + b2_ref[...]


def _post(routed, h2, x1, mod3, ws_gu, ws_down, ln_g, ln_b, seq, tm):
    n, d = h2.shape
    const = lambda a: pl.BlockSpec(a.shape, lambda i: (0,) * a.ndim, pipeline_mode=pl.Buffered(1))
    row = lambda w: pl.BlockSpec((tm, w), lambda i: (i, 0))
    return pl.pallas_call(
        _post_kernel,
        grid=(n // tm,),
        in_specs=[pl.BlockSpec((tm * SUBLANES, LANES), lambda i: (i, 0)), row(d), row(d),
                  pl.BlockSpec((1, 6, d), lambda i: (i * tm // seq, 0, 0)),
                  const(ws_gu), const(ws_down), const(ln_g), const(ln_b)],
        out_specs=row(d),
        out_shape=jax.ShapeDtypeStruct((n, d), F32),
        compiler_params=_params(("parallel",)),
        name="post",
    )(routed, h2, x1, mod3, ws_gu, ws_down, ln_g, ln_b)


def _layer(x, c, w_ada, b_ada, w_in, gla_w_gate_up, gla_b_gate, gla_norm_w, w_branch_sb, w_branch_gla,
           w_out, ln1_g, ln1_b, w_router, router_bias, w_exp_gate_up, w_exp_down, w_shared_gate_up,
           w_shared_down, ln2_g, ln2_b):
    bsz, seq, d = x.shape
    n = bsz * seq
    x2 = x.reshape(n, d)
    mod3 = _ada(c, w_ada, b_ada).reshape(bsz, 6, d)

    sbw = SB_HEADS * SB_HEAD_DIM
    gkw = GLA_HEADS * GLA_KEY_DIM
    gvw = GLA_HEADS * GLA_VALUE_DIM
    n_a = 3 * sbw + 2 * gkw + 2 * gvw
    w_a = w_in[:, :n_a].astype(BF16)
    w_glr = jnp.pad(w_in[:, n_a:n_a + GLA_GATE_RANK], ((0, 0), (0, LANES - GLA_GATE_RANK))).astype(BF16)
    w_gu = jnp.pad(gla_w_gate_up, ((0, LANES - GLA_GATE_RANK), (0, 0))).astype(BF16)
    w_merge = w_in[:, n_a + GLA_GATE_RANK:].astype(BF16)
    tm = min(512, seq)
    (q_sb, k_sb, v_sb, q_g, k_g, v_g, r_g, log_a, gate_sb, gate_gla) = _inproj(
        x2, mod3, w_a, w_glr, w_gu, gla_b_gate.reshape(1, gkw), w_merge, seq, tm)

    o_sb = _sb_attention(q_sb, k_sb, v_sb, bsz, seq, min(MXU_DIM, seq))
    o_g = _gla(q_g, k_g, v_g, log_a, r_g, gla_norm_w.reshape(1, gvw), bsz, seq)

    wr_t = w_router.T
    wr_hi = wr_t.astype(BF16)
    wr_lo = (wr_t - wr_hi.astype(F32)).astype(BF16)
    ct = min(MOE_CHUNK, n)
    x1, h2, h2row, grow, ids_t, rank_t, cnt_tiles = _mix(
        o_sb, o_g, gate_sb, gate_gla, x2, mod3,
        w_branch_sb.astype(BF16), w_branch_gla.astype(BF16), w_out.astype(BF16),
        ln1_g.reshape(1, d), ln1_b.reshape(1, d), wr_hi, wr_lo,
        router_bias.reshape(N_EXPERTS, 1), seq, tm, ct)

    off, cnt, dest = _run_offsets(cnt_tiles, ids_t, rank_t, ct, ct // tm)
    table = _dispatch(dest, ct, MOE_TM)
    routed = _moe(h2row, grow, table, off, cnt, w_exp_gate_up, w_exp_down, ct, MOE_TM)
    out = _post(routed, h2, x1, mod3, w_shared_gate_up.astype(BF16), w_shared_down.astype(BF16),
                ln2_g.reshape(1, d), ln2_b.reshape(1, d), seq, tm)
    return out.reshape(bsz, seq, d)


def kernel(x, c, w_ada, b_ada, w_in, gla_w_gate_up, gla_b_gate, gla_norm_w, w_branch_sb, w_branch_gla,
           w_out, ln1_g, ln1_b, w_router, router_bias, w_exp_gate_up, w_exp_down, w_shared_gate_up,
           w_shared_down, ln2_g, ln2_b):
    depth = w_ada.shape[0]
    for l in range(depth):
        x = _layer(x, c, w_ada[l], b_ada[l], w_in[l], gla_w_gate_up[l], gla_b_gate[l], gla_norm_w[l],
                   w_branch_sb[l], w_branch_gla[l], w_out[l], ln1_g[l], ln1_b[l], w_router[l],
                   router_bias[l], w_exp_gate_up[l], w_exp_down[l], w_shared_gate_up[l],
                   w_shared_down[l], ln2_g[l], ln2_b[l])
    return x
```
